```python
import math
import jax, jax.numpy as jnp
from jax import lax
import numpy as np

D_MODEL = 1024
BATCH = 2
SEQ = 8192
DEPTH = 1

HEAD_DIM = 64
ATTN_WIDTH = D_MODEL // 2
CONV_WIDTH = D_MODEL - ATTN_WIDTH
N_ATTN_HEADS = ATTN_WIDTH // HEAD_DIM
CONV_GROUPS = CONV_WIDTH // HEAD_DIM
CONV_KERNEL = 31
D_FF = 4 * D_MODEL
Q_BLOCK = 128
LN_EPS = 1e-5
DEEPNORM_ALPHA = (2.0 * DEPTH) ** 0.25
DEEPNORM_BETA = (8.0 * DEPTH) ** -0.25
IN_SPLITS = [ATTN_WIDTH, 2 * ATTN_WIDTH, 3 * ATTN_WIDTH, 3 * ATTN_WIDTH + N_ATTN_HEADS,
             3 * ATTN_WIDTH + N_ATTN_HEADS + CONV_WIDTH]
N_IN_COLS = 3 * ATTN_WIDTH + N_ATTN_HEADS + 2 * CONV_WIDTH

kernel_name = "hymba_fox_conformer_deepnorm_adaln_block"


def _layernorm(x, g, b):
    xf = x.astype(jnp.float32)
    mu = jnp.mean(xf, axis=-1, keepdims=True)
    var = jnp.mean(jnp.square(xf - mu), axis=-1, keepdims=True)
    return ((xf - mu) * lax.rsqrt(var + LN_EPS)).astype(x.dtype) * g + b


def _rmsnorm(x, g):
    xf = x.astype(jnp.float32)
    return (xf * lax.rsqrt(jnp.mean(xf * xf, axis=-1, keepdims=True) + LN_EPS)).astype(x.dtype) * g


def _forgetting_attention(q, k, v, log_f):
    b, h, s, dh = q.shape
    n_blk = s // Q_BLOCK
    cum = jnp.cumsum(log_f, axis=-1)
    q_blocks = (q * (dh ** -0.5)).reshape(b, h, n_blk, Q_BLOCK, dh).transpose(2, 0, 1, 3, 4)
    cq_blocks = cum.reshape(b, h, n_blk, Q_BLOCK).transpose(2, 0, 1, 3)
    k_pos = jnp.arange(s)

    def one_block(args):
        qb, cqb, blk = args
        q_pos = blk * Q_BLOCK + jnp.arange(Q_BLOCK)
        logits = jnp.einsum('bhqd,bhkd->bhqk', qb, k).astype(jnp.float32)
        logits = logits + cqb[..., :, None] - cum[..., None, :]
        causal = k_pos[None, :] <= q_pos[:, None]
        logits = jnp.where(causal, logits, -jnp.inf)
        p = jax.nn.softmax(logits, axis=-1)
        return jnp.einsum('bhqk,bhkd->bhqd', p.astype(v.dtype), v)

    out = lax.map(one_block, (q_blocks, cq_blocks, jnp.arange(n_blk)))
    return out.transpose(1, 2, 0, 3, 4).reshape(b, h, s, dh)


def _conformer_conv(a, gate, w_dw, b_dw, gn_g, gn_b):
    u = a * jax.nn.sigmoid(gate)
    y = lax.conv_general_dilated(u, w_dw, window_strides=(1,), padding=((CONV_KERNEL - 1, 0),),
                                 dimension_numbers=('NWC', 'WIO', 'NWC'),
                                 feature_group_count=CONV_WIDTH) + b_dw
    bsz, seq, ch = y.shape
    yg = y.reshape(bsz, seq, CONV_GROUPS, ch // CONV_GROUPS).astype(jnp.float32)
    mu = jnp.mean(yg, axis=-1, keepdims=True)
    var = jnp.mean(jnp.square(yg - mu), axis=-1, keepdims=True)
    yn = ((yg - mu) * lax.rsqrt(var + LN_EPS)).reshape(bsz, seq, ch).astype(y.dtype) * gn_g + gn_b
    return jax.nn.silu(yn)


def setup_inputs(seed: int = 0) -> dict:
    key = jax.random.key(seed)
    ks = jax.random.split(key, 24)
    nrm = jax.random.normal
    d = D_MODEL
    x = nrm(ks[0], (BATCH, SEQ, d), jnp.float32)
    c = nrm(ks[1], (BATCH, d), jnp.float32)
    w_ada = nrm(ks[2], (DEPTH, d, 6 * d), jnp.float32) * (0.1 * d ** -0.5)
    b_ada = nrm(ks[3], (DEPTH, 6 * d), jnp.float32) * 0.02
    col_scale = jnp.concatenate([
        jnp.ones((2 * ATTN_WIDTH,), jnp.float32),
        jnp.full((ATTN_WIDTH,), DEEPNORM_BETA, jnp.float32),
        jnp.ones((N_ATTN_HEADS,), jnp.float32),
        jnp.full((CONV_WIDTH,), DEEPNORM_BETA, jnp.float32),
        jnp.ones((CONV_WIDTH,), jnp.float32)])
    w_in = nrm(ks[4], (DEPTH, d, N_IN_COLS), jnp.float32) * (d ** -0.5) * col_scale
    b_forget = jax.random.uniform(ks[5], (DEPTH, N_ATTN_HEADS), jnp.float32, minval=2.0, maxval=6.0)
    w_dw = nrm(ks[6], (DEPTH, CONV_KERNEL, 1, CONV_WIDTH), jnp.float32) * (CONV_KERNEL ** -0.5)
    b_dw = nrm(ks[7], (DEPTH, CONV_WIDTH), jnp.float32) * 0.02
    gn_g = 1.0 + 0.02 * nrm(ks[8], (DEPTH, CONV_WIDTH), jnp.float32)
    gn_b = 0.02 * nrm(ks[9], (DEPTH, CONV_WIDTH), jnp.float32)
    g_attn_out = 1.0 + 0.02 * nrm(ks[10], (DEPTH, ATTN_WIDTH), jnp.float32)
    g_conv_out = 1.0 + 0.02 * nrm(ks[11], (DEPTH, CONV_WIDTH), jnp.float32)
    w_out = nrm(ks[12], (DEPTH, d, d), jnp.float32) * (d ** -0.5) * DEEPNORM_BETA
    ln1_g = 1.0 + 0.02 * nrm(ks[13], (DEPTH, d), jnp.float32)
    ln1_b = 0.02 * nrm(ks[14], (DEPTH, d), jnp.float32)
    w_ff1 = nrm(ks[15], (DEPTH, d, D_FF), jnp.float32) * (d ** -0.5) * DEEPNORM_BETA
    w_ff2 = nrm(ks[16], (DEPTH, D_FF, d), jnp.float32) * (D_FF ** -0.5) * DEEPNORM_BETA
    ln2_g = 1.0 + 0.02 * nrm(ks[17], (DEPTH, d), jnp.float32)
    ln2_b = 0.02 * nrm(ks[18], (DEPTH, d), jnp.float32)
    return {"x": x, "c": c, "w_ada": w_ada, "b_ada": b_ada, "w_in": w_in, "b_forget": b_forget,
            "w_dw": w_dw, "b_dw": b_dw, "gn_g": gn_g, "gn_b": gn_b, "g_attn_out": g_attn_out,
            "g_conv_out": g_conv_out, "w_out": w_out, "ln1_g": ln1_g, "ln1_b": ln1_b,
            "w_ff1": w_ff1, "w_ff2": w_ff2, "ln2_g": ln2_g, "ln2_b": ln2_b}


def reference(x, c, w_ada, b_ada, w_in, b_forget, w_dw, b_dw, gn_g, gn_b, g_attn_out,
              g_conv_out, w_out, ln1_g, ln1_b, w_ff1, w_ff2, ln2_g, ln2_b):
    bsz, seq, _ = x.shape
    for layer in range(DEPTH):
        ada = jax.nn.silu(c) @ w_ada[layer] + b_ada[layer]
        sh1, sc1, gt1, sh2, sc2, gt2 = jnp.split(ada[:, None, :], 6, axis=-1)

        u = x * (1 + sc1) + sh1
        proj = u @ w_in[layer]
        q, k, v, f_logit, a, g = jnp.split(proj, IN_SPLITS, axis=-1)

        def heads(t):
            return t.reshape(bsz, seq, N_ATTN_HEADS, HEAD_DIM).transpose(0, 2, 1, 3)

        log_f = jax.nn.log_sigmoid((f_logit + b_forget[layer]).astype(jnp.float32)).transpose(0, 2, 1)
        attn = _forgetting_attention(heads(q), heads(k), heads(v), log_f)
        attn = attn.transpose(0, 2, 1, 3).reshape(bsz, seq, ATTN_WIDTH)
        conv = _conformer_conv(a, g, w_dw[layer], b_dw[layer], gn_g[layer], gn_b[layer])

        mixed = jnp.concatenate([_rmsnorm(attn, g_attn_out[layer]),
                                 _rmsnorm(conv, g_conv_out[layer])], axis=-1) @ w_out[layer]
        x = _layernorm(DEEPNORM_ALPHA * x + (1 + gt1) * mixed, ln1_g[layer], ln1_b[layer])

        u2 = x * (1 + sc2) + sh2
        hid = jnp.square(jax.nn.relu(u2 @ w_ff1[layer]))
        ff = hid @ w_ff2[layer]
        x = _layernorm(DEEPNORM_ALPHA * x + (1 + gt2) * ff, ln2_g[layer], ln2_b[layer])
    return x
```

```python
import functools

import numpy as np
import jax
import jax.numpy as jnp
from jax import lax
from jax.experimental import pallas as pl
from jax.experimental.pallas import tpu as pltpu

D_MODEL = 1024
HEAD_DIM = 64
ATTN_WIDTH = 512
CONV_WIDTH = 512
N_HEADS = 8
N_PAIRS = N_HEADS // 2
CONV_KERNEL = 31
CONV_GROUP = 64
D_FF = 4 * D_MODEL
LN_EPS = 1e-5
DEEPNORM_ALPHA = 2.0 ** 0.25

LANES = 128
HALO = 32
AUG_PER_HEAD = 6
PIECE_LANE = (0, 8, 16)
ONE_LANE = 24
NEG_BIG = -1e30

TM_PROJ = 512
TQ = 256
TK = 256
VMEM_LIMIT = 56 * 1024 * 1024

F32 = jnp.float32
BF16 = jnp.bfloat16


def _split3(x):
    hi = x.astype(BF16)
    r = x - hi.astype(F32)
    mid = r.astype(BF16)
    lo = (r - mid.astype(F32)).astype(BF16)
    return hi, mid, lo


def _split2(x):
    hi = x.astype(BF16)
    lo = (x - hi.astype(F32)).astype(BF16)
    return hi, lo


def _dot(a, b):
    return jnp.dot(a, b, preferred_element_type=F32)


def _ada_kernel(c_ref, w_ref, b_ref, o_ref):
    c = c_ref[...]
    s = c * jax.nn.sigmoid(c)
    o_ref[...] = jnp.dot(s, w_ref[...], preferred_element_type=F32,
                         precision=lax.Precision.HIGHEST) + b_ref[...]


def _ada(c_pad, w_ada, b_ada):
    n = w_ada.shape[1]
    tn = 1536
    return pl.pallas_call(
        _ada_kernel,
        out_shape=jax.ShapeDtypeStruct((c_pad.shape[0], n), F32),
        grid=(n // tn,),
        in_specs=[pl.BlockSpec(c_pad.shape, lambda j: (0, 0)),
                  pl.BlockSpec((D_MODEL, tn), lambda j: (0, j)),
                  pl.BlockSpec((1, tn), lambda j: (0, j))],
        out_specs=pl.BlockSpec((c_pad.shape[0], tn), lambda j: (0, j)),
        name="ada",
    )(c_pad, w_ada, b_ada)


def _inproj_kernel(x_ref, mod_ref, wqkv_ref, wf_ref, bf_ref, wa_ref, wg_ref, psel_ref,
                   qkv_ref, augq_ref, augk_ref, uc_ref, carry_ref):
    tm = x_ref.shape[1]

    @pl.when(pl.program_id(1) == 0)
    def _():
        carry_ref[...] = jnp.zeros_like(carry_ref)

    x = x_ref[0]
    shift = mod_ref[0, 0:1, :]
    scale = mod_ref[0, 1:2, :]
    u = (x * (1.0 + scale) + shift).astype(BF16)

    qkv = _dot(u, wqkv_ref[...])
    qkv_ref[0, :, :ATTN_WIDTH] = (qkv[:, :ATTN_WIDTH] * (HEAD_DIM ** -0.5)).astype(BF16)
    qkv_ref[0, :, ATTN_WIDTH:] = qkv[:, ATTN_WIDTH:].astype(BF16)

    fl = _dot(u, wf_ref[...]) + bf_ref[...]
    log_f = jnp.minimum(fl, 0.0) - jnp.log(1.0 + jnp.exp(-jnp.abs(fl)))

    row = lax.broadcasted_iota(jnp.int32, (tm, tm), 0)
    col = lax.broadcasted_iota(jnp.int32, (tm, tm), 1)
    tri = jnp.where(row >= col, 1.0, 0.0).astype(BF16)
    h, m, l = _split3(log_f)
    cum = carry_ref[...] + ((_dot(tri, h) + _dot(tri, m)) + _dot(tri, l))
    carry_ref[...] = cum[tm - 1:tm, :]

    lane = lax.broadcasted_iota(jnp.int32, (tm, LANES), 1)
    ch, cm, cl = (t.astype(F32) for t in _split3(cum))
    pieces = jnp.where(lane < PIECE_LANE[1], ch,
                       jnp.where(lane < PIECE_LANE[2], cm,
                                 jnp.where(lane < ONE_LANE, cl,
                                           jnp.where(lane == ONE_LANE, 1.0, 0.0))))
    aug = _dot(pieces.astype(BF16), psel_ref[...]).astype(BF16)
    for p in range(N_PAIRS):
        augq_ref[0, p] = aug[:, p * LANES:(p + 1) * LANES]
        augk_ref[0, p] = aug[:, (N_PAIRS + p) * LANES:(N_PAIRS + p + 1) * LANES]

    a = _dot(u, wa_ref[...])
    g = _dot(u, wg_ref[...])
    uc_ref[0] = a * jax.nn.sigmoid(g)


def _decay_routing_matrix():
    sel = np.zeros((LANES, 2 * N_PAIRS * LANES), np.float32)
    for p in range(N_PAIRS):
        for j in range(2):
            head = 2 * p + j
            qbase = p * LANES + AUG_PER_HEAD * j
            kbase = (N_PAIRS + p) * LANES + AUG_PER_HEAD * j
            for i in range(3):
                sel[PIECE_LANE[i] + head, qbase + i] = 1.0
                sel[ONE_LANE, qbase + 3 + i] = 1.0
                sel[ONE_LANE, kbase + i] = 1.0
                sel[PIECE_LANE[i] + head, kbase + 3 + i] = -1.0
    return jnp.asarray(sel, BF16)


def _inproj(x, mod1, wqkv, wf, bf, wa, wg, psel):
    b, s, d = x.shape
    tm = TM_PROJ
    const = lambda shape: pl.BlockSpec(shape, lambda bi, i: (0,) * len(shape))
    return pl.pallas_call(
        _inproj_kernel,
        out_shape=(jax.ShapeDtypeStruct((b, s, 3 * ATTN_WIDTH), BF16),
                   jax.ShapeDtypeStruct((b, N_PAIRS, s, LANES), BF16),
                   jax.ShapeDtypeStruct((b, N_PAIRS, s, LANES), BF16),
                   jax.ShapeDtypeStruct((b, s, CONV_WIDTH), F32)),
        grid=(b, s // tm),
        in_specs=[pl.BlockSpec((1, tm, d), lambda bi, i: (bi, i, 0)),
                  pl.BlockSpec((1, 3, d), lambda bi, i: (bi, 0, 0)),
                  const(wqkv.shape), const(wf.shape), const(bf.shape),
                  const(wa.shape), const(wg.shape), const(psel.shape)],
        out_specs=(pl.BlockSpec((1, tm, 3 * ATTN_WIDTH), lambda bi, i: (bi, i, 0)),
                   pl.BlockSpec((1, N_PAIRS, tm, LANES), lambda bi, i: (bi, 0, i, 0)),
                   pl.BlockSpec((1, N_PAIRS, tm, LANES), lambda bi, i: (bi, 0, i, 0)),
                   pl.BlockSpec((1, tm, CONV_WIDTH), lambda bi, i: (bi, i, 0))),
        scratch_shapes=[pltpu.VMEM((1, LANES), F32)],
        compiler_params=pltpu.CompilerParams(
            dimension_semantics=("arbitrary", "arbitrary"), vmem_limit_bytes=VMEM_LIMIT),
        name="inproj",
    )(x, mod1, wqkv, wf, bf, wa, wg, psel)


def _conv_kernel(uc_ref, wdw_ref, bdw_ref, gng_ref, gnb_ref, gout_ref, gmat_ref,
                 o_ref, ext_ref):
    tm = uc_ref.shape[1]

    @pl.when(pl.program_id(1) == 0)
    def _():
        ext_ref[0:HALO, :] = jnp.zeros((HALO, CONV_WIDTH), F32)

    ext_ref[HALO:, :] = uc_ref[0]
    first = HALO - (CONV_KERNEL - 1)
    y = jnp.zeros((tm, CONV_WIDTH), F32) + bdw_ref[...]
    for k in range(CONV_KERNEL):
        y = y + wdw_ref[k:k + 1, :] * ext_ref[first + k:first + k + tm, :]
    ext_ref[0:HALO, :] = ext_ref[tm:tm + HALO, :]

    gmat = gmat_ref[...]
    yh, yl = _split2(y)
    mu = _dot(yh, gmat) + _dot(yl, gmat)
    d = y - mu
    dh, dl = _split2(d * d)
    var = _dot(dh, gmat) + _dot(dl, gmat)
    yn = d * lax.rsqrt(var + LN_EPS) * gng_ref[...] + gnb_ref[...]
    sw = yn * jax.nn.sigmoid(yn)
    ms = jnp.mean(sw * sw, axis=-1, keepdims=True)
    o_ref[0] = (sw * lax.rsqrt(ms + LN_EPS) * gout_ref[...]).astype(BF16)


def _conv(uc, wdw, bdw, gng, gnb, gout, gmat):
    b, s, c = uc.shape
    tm = TM_PROJ
    const = lambda shape: pl.BlockSpec(shape, lambda bi, i: (0,) * len(shape))
    return pl.pallas_call(
        _conv_kernel,
        out_shape=jax.ShapeDtypeStruct((b, s, c), BF16),
        grid=(b, s // tm),
        in_specs=[pl.BlockSpec((1, tm, c), lambda bi, i: (bi, i, 0)),
                  const(wdw.shape), const(bdw.shape), const(gng.shape), const(gnb.shape),
                  const(gout.shape), const(gmat.shape)],
        out_specs=pl.BlockSpec((1, tm, c), lambda bi, i: (bi, i, 0)),
        scratch_shapes=[pltpu.VMEM((HALO + tm, c), F32)],
        compiler_params=pltpu.CompilerParams(
            dimension_semantics=("arbitrary", "arbitrary"), vmem_limit_bytes=VMEM_LIMIT),
        name="conv",
    )(uc, wdw, bdw, gng, gnb, gout, gmat)


def _attn_kernel(q_ref, aq_ref, k_ref, ak_ref, v_ref, o_ref):
    qi = pl.program_id(2)
    tq = q_ref.shape[1]
    nt = (((1,), (1,)), ((), ()))

    qcat = jnp.concatenate([q_ref[0], aq_ref[0, 0]], axis=-1)
    lane = lax.broadcasted_iota(jnp.int32, (1, 2 * LANES), 1)
    qh = []
    for j in range(2):
        keep = (((lane >= HEAD_DIM * j) & (lane < HEAD_DIM * (j + 1))) |
                ((lane >= LANES + AUG_PER_HEAD * j) & (lane < LANES + AUG_PER_HEAD * (j + 1))))
        qh.append(qcat * jnp.where(keep, 1.0, 0.0).astype(BF16))

    def block(kb, carry, masked):
        start = pl.multiple_of(kb * TK, TK)
        kcat = jnp.concatenate([k_ref[0, pl.ds(start, TK), :], ak_ref[0, 0, pl.ds(start, TK), :]],
                               axis=-1)
        vb = v_ref[0, pl.ds(start, TK), :]
        out = []
        for j in range(2):
            m, l, acc = carry[j]
            s = lax.dot_general(qh[j], kcat, nt, preferred_element_type=F32)
            if masked:
                r = lax.broadcasted_iota(jnp.int32, (tq, TK), 0)
                c = lax.broadcasted_iota(jnp.int32, (tq, TK), 1)
                s = jnp.where(r >= c, s, NEG_BIG)
            m_new = jnp.maximum(m, jnp.max(s, axis=-1, keepdims=True))
            alpha = jnp.exp(m - m_new)
            p = jnp.exp(s - m_new)
            l = alpha * l + jnp.sum(p, axis=-1, keepdims=True)
            acc = alpha * acc + _dot(p.astype(BF16), vb)
            out.append((m_new, l, acc))
        return tuple(out)

    init = tuple((jnp.full((tq, 1), NEG_BIG, F32), jnp.zeros((tq, 1), F32),
                  jnp.zeros((tq, LANES), F32)) for _ in range(2))
    carry = lax.fori_loop(0, qi, lambda kb, c: block(kb, c, False), init)
    carry = block(qi, carry, True)

    out0 = carry[0][2] / carry[0][1]
    out1 = carry[1][2] / carry[1][1]
    lane_o = lax.broadcasted_iota(jnp.int32, (1, LANES), 1)
    o_ref[0] = jnp.where(lane_o < HEAD_DIM, out0, out1)


def _attn(qkv, augq, augk):
    b, s, _ = qkv.shape
    return pl.pallas_call(
        _attn_kernel,
        out_shape=jax.ShapeDtypeStruct((b, s, ATTN_WIDTH), F32),
        grid=(b, N_PAIRS, s // TQ),
        in_specs=[pl.BlockSpec((1, TQ, LANES), lambda bi, p, i: (bi, i, p)),
                  pl.BlockSpec((1, 1, TQ, LANES), lambda bi, p, i: (bi, p, i, 0)),
                  pl.BlockSpec((1, s, LANES), lambda bi, p, i: (bi, 0, N_PAIRS + p)),
                  pl.BlockSpec((1, 1, s, LANES), lambda bi, p, i: (bi, p, 0, 0)),
                  pl.BlockSpec((1, s, LANES), lambda bi, p, i: (bi, 0, 2 * N_PAIRS + p))],
        out_specs=pl.BlockSpec((1, TQ, LANES), lambda bi, p, i: (bi, i, p)),
        compiler_params=pltpu.CompilerParams(
            dimension_semantics=("arbitrary", "arbitrary", "arbitrary"),
            vmem_limit_bytes=VMEM_LIMIT),
        name="attn",
    )(qkv, augq, qkv, augk, qkv)


def _layernorm(y, g, b):
    mu = jnp.mean(y, axis=-1, keepdims=True)
    d = y - mu
    var = jnp.mean(d * d, axis=-1, keepdims=True)
    return d * lax.rsqrt(var + LN_EPS) * g + b


def _outproj_kernel(attn_ref, cn_ref, x_ref, mod_ref, gattn_ref, wo_a_ref, wo_c_ref,
                    lng_ref, lnb_ref, o_ref):
    a = attn_ref[0]
    ms = jnp.mean(a * a, axis=-1, keepdims=True)
    an = (a * lax.rsqrt(ms + LN_EPS) * gattn_ref[...]).astype(BF16)
    mixed = _dot(an, wo_a_ref[...]) + _dot(cn_ref[0], wo_c_ref[...])
    gate = mod_ref[0, 2:3, :]
    y = DEEPNORM_ALPHA * x_ref[0] + (1.0 + gate) * mixed
    o_ref[0] = _layernorm(y, lng_ref[...], lnb_ref[...])


def _outproj(attn, cn, x, mod1, gattn, wo_a, wo_c, lng, lnb):
    b, s, d = x.shape
    tm = TM_PROJ
    const = lambda shape: pl.BlockSpec(shape, lambda bi, i: (0,) * len(shape))
    return pl.pallas_call(
        _outproj_kernel,
        out_shape=jax.ShapeDtypeStruct((b, s, d), F32),
        grid=(b, s // tm),
        in_specs=[pl.BlockSpec((1, tm, ATTN_WIDTH), lambda bi, i: (bi, i, 0)),
                  pl.BlockSpec((1, tm, CONV_WIDTH), lambda bi, i: (bi, i, 0)),
                  pl.BlockSpec((1, tm, d), lambda bi, i: (bi, i, 0)),
                  pl.BlockSpec((1, 3, d), lambda bi, i: (bi, 0, 0)),
                  const(gattn.shape), const(wo_a.shape), const(wo_c.shape),
                  const(lng.shape), const(lnb.shape)],
        out_specs=pl.BlockSpec((1, tm, d), lambda bi, i: (bi, i, 0)),
        compiler_params=pltpu.CompilerParams(
            dimension_semantics=("arbitrary", "arbitrary"), vmem_limit_bytes=VMEM_LIMIT),
        name="outproj",
    )(attn, cn, x, mod1, gattn, wo_a, wo_c, lng, lnb)


def _ffn_kernel(x_ref, mod_ref, w1_ref, w2_ref, lng_ref, lnb_ref, o_ref):
    x = x_ref[0]
    shift = mod_ref[0, 0:1, :]
    scale = mod_ref[0, 1:2, :]
    gate = mod_ref[0, 2:3, :]
    u = (x * (1.0 + scale) + shift).astype(BF16)
    ff = jnp.zeros(x.shape, F32)
    chunk = D_MODEL
    for c in range(D_FF // chunk):
        hid = jnp.maximum(_dot(u, w1_ref[:, c * chunk:(c + 1) * chunk]), 0.0)
        ff = ff + _dot((hid * hid).astype(BF16), w2_ref[c * chunk:(c + 1) * chunk, :])
    y = DEEPNORM_ALPHA * x + (1.0 + gate) * ff
    o_ref[0] = _layernorm(y, lng_ref[...], lnb_ref[...])


def _ffn(x1, mod2, w1, w2, lng, lnb):
    b, s, d = x1.shape
    tm = TM_PROJ
    const = lambda shape: pl.BlockSpec(shape, lambda bi, i: (0,) * len(shape),
                                       pipeline_mode=pl.Buffered(1))
    return pl.pallas_call(
        _ffn_kernel,
        out_shape=jax.ShapeDtypeStruct((b, s, d), F32),
        grid=(b, s // tm),
        in_specs=[pl.BlockSpec((1, tm, d), lambda bi, i: (bi, i, 0)),
                  pl.BlockSpec((1, 3, d), lambda bi, i: (bi, 0, 0)),
                  const(w1.shape), const(w2.shape), const(lng.shape), const(lnb.shape)],
        out_specs=pl.BlockSpec((1, tm, d), lambda bi, i: (bi, i, 0)),
        compiler_params=pltpu.CompilerParams(
            dimension_semantics=("arbitrary", "arbitrary"), vmem_limit_bytes=VMEM_LIMIT),
        name="ffn",
    )(x1, mod2, w1, w2, lng, lnb)


def kernel(x, c, w_ada, b_ada, w_in, b_forget, w_dw, b_dw, gn_g, gn_b, g_attn_out, g_conv_out,
           w_out, ln1_g, ln1_b, w_ff1, w_ff2, ln2_g, ln2_b):
    bsz = x.shape[0]
    layer = 0
    row = lambda v: v.reshape(1, -1)

    c_pad = jnp.pad(c, ((0, 8 - bsz), (0, 0)))
    ada = _ada(c_pad, w_ada[layer], row(b_ada[layer]))[:bsz]
    mods = ada.reshape(bsz, 6, D_MODEL)
    mod1, mod2 = mods[:, 0:3], mods[:, 3:6]

    w = w_in[layer]
    a0 = 3 * ATTN_WIDTH + N_HEADS
    wqkv = w[:, :3 * ATTN_WIDTH].astype(BF16)
    reps = LANES // N_HEADS
    wf = jnp.tile(w[:, 3 * ATTN_WIDTH:a0], (1, reps)).astype(BF16)
    bf = jnp.tile(b_forget[layer], reps).reshape(1, LANES)
    wa = w[:, a0:a0 + CONV_WIDTH].astype(BF16)
    wg = w[:, a0 + CONV_WIDTH:].astype(BF16)

    qkv, augq, augk, uc = _inproj(x, mod1, wqkv, wf, bf, wa, wg, _decay_routing_matrix())

    grp = np.arange(CONV_WIDTH) // CONV_GROUP
    gmat = jnp.asarray((grp[:, None] == grp[None, :]).astype(np.float32) / CONV_GROUP, BF16)
    cn = _conv(uc, w_dw[layer].reshape(CONV_KERNEL, CONV_WIDTH), row(b_dw[layer]),
               row(gn_g[layer]), row(gn_b[layer]), row(g_conv_out[layer]), gmat)

    attn = _attn(qkv, augq, augk)

    wo = w_out[layer].astype(BF16)
    x1 = _outproj(attn, cn, x, mod1, row(g_attn_out[layer]), wo[:ATTN_WIDTH], wo[ATTN_WIDTH:],
                  row(ln1_g[layer]), row(ln1_b[layer]))

    return _ffn(x1, mod2, w_ff1[layer].astype(BF16), w_ff2[layer].astype(BF16),
                row(ln2_g[layer]), row(ln2_b[layer]))
```

```python
import functools

import numpy as np
import jax
import jax.numpy as jnp
from jax import lax
from jax.experimental import pallas as pl
from jax.experimental.pallas import tpu as pltpu

D_MODEL = 1024
HEAD_DIM = 64
ATTN_WIDTH = 512
CONV_WIDTH = 512
N_HEADS = 8
N_PAIRS = N_HEADS // 2
CONV_KERNEL = 31
CONV_GROUP = 64
D_FF = 4 * D_MODEL
LN_EPS = 1e-5
DEEPNORM_ALPHA = 2.0 ** 0.25

LANES = 128
HALO = 32
AUG_PER_HEAD = 6
AUG_ROWS = 16
NT_DIMS = (((1,), (1,)), ((), ()))
PIECE_LANE = (0, 8, 16)
ONE_LANE = 24
NEG_BIG = -1e30

TM_PROJ = 512
TQ = 256
TK = 256
VMEM_LIMIT = 56 * 1024 * 1024

F32 = jnp.float32
BF16 = jnp.bfloat16


def _split3(x):
    hi = x.astype(BF16)
    r = x - hi.astype(F32)
    mid = r.astype(BF16)
    lo = (r - mid.astype(F32)).astype(BF16)
    return hi, mid, lo


def _split2(x):
    hi = x.astype(BF16)
    lo = (x - hi.astype(F32)).astype(BF16)
    return hi, lo


def _dot(a, b):
    return jnp.dot(a, b, preferred_element_type=F32)


def _ada_kernel(c_ref, w_ref, b_ref, o_ref):
    c = c_ref[...]
    s = c * jax.nn.sigmoid(c)
    o_ref[...] = jnp.dot(s, w_ref[...], preferred_element_type=F32,
                         precision=lax.Precision.HIGHEST) + b_ref[...]


def _ada(c_pad, w_ada, b_ada):
    n = w_ada.shape[1]
    tn = 1536
    return pl.pallas_call(
        _ada_kernel,
        out_shape=jax.ShapeDtypeStruct((c_pad.shape[0], n), F32),
        grid=(n // tn,),
        in_specs=[pl.BlockSpec(c_pad.shape, lambda j: (0, 0)),
                  pl.BlockSpec((D_MODEL, tn), lambda j: (0, j)),
                  pl.BlockSpec((1, tn), lambda j: (0, j))],
        out_specs=pl.BlockSpec((c_pad.shape[0], tn), lambda j: (0, j)),
        name="ada",
    )(c_pad, w_ada, b_ada)


def _inproj_kernel(x_ref, mod_ref, wqv_t_ref, wk_ref, wf_ref, bf_ref, wa_ref, wg_ref,
                   pselq_t_ref, pselk_ref,
                   qt_ref, vt_ref, k_ref, augq_t_ref, augk_ref, uc_ref, carry_ref):
    tm = x_ref.shape[1]

    @pl.when(pl.program_id(1) == 0)
    def _():
        carry_ref[...] = jnp.zeros_like(carry_ref)

    x = x_ref[0]
    shift = mod_ref[0, 0:1, :]
    scale = mod_ref[0, 1:2, :]
    u = (x * (1.0 + scale) + shift).astype(BF16)

    qv_t = lax.dot_general(wqv_t_ref[...], u, NT_DIMS, preferred_element_type=F32)
    qt_ref[0] = (qv_t[:ATTN_WIDTH] * (HEAD_DIM ** -0.5)).astype(BF16)
    for t in range(tm // TK):
        vt_ref[0, t] = qv_t[ATTN_WIDTH:, t * TK:(t + 1) * TK].astype(BF16)
    k_ref[0] = _dot(u, wk_ref[...]).astype(BF16)

    fl = _dot(u, wf_ref[...]) + bf_ref[...]
    log_f = jnp.minimum(fl, 0.0) - jnp.log(1.0 + jnp.exp(-jnp.abs(fl)))

    row = lax.broadcasted_iota(jnp.int32, (tm, tm), 0)
    col = lax.broadcasted_iota(jnp.int32, (tm, tm), 1)
    tri = jnp.where(row >= col, 1.0, 0.0).astype(BF16)
    h, m, l = _split3(log_f)
    cum = carry_ref[...] + ((_dot(tri, h) + _dot(tri, m)) + _dot(tri, l))
    carry_ref[...] = cum[tm - 1:tm, :]

    lane = lax.broadcasted_iota(jnp.int32, (tm, LANES), 1)
    ch, cm, cl = (t.astype(F32) for t in _split3(cum))
    pieces = jnp.where(lane < PIECE_LANE[1], ch,
                       jnp.where(lane < PIECE_LANE[2], cm,
                                 jnp.where(lane < ONE_LANE, cl,
                                           jnp.where(lane == ONE_LANE, 1.0, 0.0))))
    pieces = pieces.astype(BF16)
    augk_ref[0] = _dot(pieces, pselk_ref[...]).astype(BF16)
    augq_t_ref[0] = lax.dot_general(pselq_t_ref[...], pieces, NT_DIMS,
                                    preferred_element_type=F32).astype(BF16)

    a = _dot(u, wa_ref[...])
    g = _dot(u, wg_ref[...])
    uc_ref[0] = a * jax.nn.sigmoid(g)


def _decay_routing_matrices():
    selq_t = np.zeros((N_HEADS * AUG_ROWS, LANES), np.float32)
    selk = np.zeros((LANES, N_PAIRS * LANES), np.float32)
    for head in range(N_HEADS):
        p, j = divmod(head, 2)
        qbase = head * AUG_ROWS + AUG_PER_HEAD * j
        kbase = p * LANES + AUG_PER_HEAD * j
        for i in range(3):
            selq_t[qbase + i, PIECE_LANE[i] + head] = 1.0
            selq_t[qbase + 3 + i, ONE_LANE] = 1.0
            selk[ONE_LANE, kbase + i] = 1.0
            selk[PIECE_LANE[i] + head, kbase + 3 + i] = -1.0
    return jnp.asarray(selq_t, BF16), jnp.asarray(selk, BF16)


def _inproj(x, mod1, wqv_t, wk, wf, bf, wa, wg, pselq_t, pselk):
    b, s, d = x.shape
    tm = TM_PROJ
    const = lambda shape: pl.BlockSpec(shape, lambda bi, i: (0,) * len(shape))
    return pl.pallas_call(
        _inproj_kernel,
        out_shape=(jax.ShapeDtypeStruct((b, ATTN_WIDTH, s), BF16),
                   jax.ShapeDtypeStruct((b, s // TK, ATTN_WIDTH, TK), BF16),
                   jax.ShapeDtypeStruct((b, s, ATTN_WIDTH), BF16),
                   jax.ShapeDtypeStruct((b, N_HEADS * AUG_ROWS, s), BF16),
                   jax.ShapeDtypeStruct((b, s, N_PAIRS * LANES), BF16),
                   jax.ShapeDtypeStruct((b, s, CONV_WIDTH), F32)),
        grid=(b, s // tm),
        in_specs=[pl.BlockSpec((1, tm, d), lambda bi, i: (bi, i, 0)),
                  pl.BlockSpec((1, 3, d), lambda bi, i: (bi, 0, 0)),
                  const(wqv_t.shape), const(wk.shape), const(wf.shape), const(bf.shape),
                  const(wa.shape), const(wg.shape), const(pselq_t.shape), const(pselk.shape)],
        out_specs=(pl.BlockSpec((1, ATTN_WIDTH, tm), lambda bi, i: (bi, 0, i)),
                   pl.BlockSpec((1, tm // TK, ATTN_WIDTH, TK), lambda bi, i: (bi, i, 0, 0)),
                   pl.BlockSpec((1, tm, ATTN_WIDTH), lambda bi, i: (bi, i, 0)),
                   pl.BlockSpec((1, N_HEADS * AUG_ROWS, tm), lambda bi, i: (bi, 0, i)),
                   pl.BlockSpec((1, tm, N_PAIRS * LANES), lambda bi, i: (bi, i, 0)),
                   pl.BlockSpec((1, tm, CONV_WIDTH), lambda bi, i: (bi, i, 0))),
        scratch_shapes=[pltpu.VMEM((1, LANES), F32)],
        compiler_params=pltpu.CompilerParams(
            dimension_semantics=("arbitrary", "arbitrary"), vmem_limit_bytes=VMEM_LIMIT),
        name="inproj",
    )(x, mod1, wqv_t, wk, wf, bf, wa, wg, pselq_t, pselk)


def _conv_kernel(uc_ref, wdw_ref, bdw_ref, gng_ref, gnb_ref, gout_ref, gmat_ref,
                 o_ref, ext_ref):
    tm = uc_ref.shape[1]

    @pl.when(pl.program_id(1) == 0)
    def _():
        ext_ref[0:HALO, :] = jnp.zeros((HALO, CONV_WIDTH), F32)

    ext_ref[HALO:, :] = uc_ref[0]
    first = HALO - (CONV_KERNEL - 1)
    y = jnp.zeros((tm, CONV_WIDTH), F32) + bdw_ref[...]
    for k in range(CONV_KERNEL):
        y = y + wdw_ref[k:k + 1, :] * ext_ref[first + k:first + k + tm, :]
    ext_ref[0:HALO, :] = ext_ref[tm:tm + HALO, :]

    gmat = gmat_ref[...]
    yh, yl = _split2(y)
    mu = _dot(yh, gmat) + _dot(yl, gmat)
    d = y - mu
    dh, dl = _split2(d * d)
    var = _dot(dh, gmat) + _dot(dl, gmat)
    yn = d * lax.rsqrt(var + LN_EPS) * gng_ref[...] + gnb_ref[...]
    sw = yn * jax.nn.sigmoid(yn)
    ms = jnp.mean(sw * sw, axis=-1, keepdims=True)
    o_ref[0] = (sw * lax.rsqrt(ms + LN_EPS) * gout_ref[...]).astype(BF16)


def _conv(uc, wdw, bdw, gng, gnb, gout, gmat):
    b, s, c = uc.shape
    tm = TM_PROJ
    const = lambda shape: pl.BlockSpec(shape, lambda bi, i: (0,) * len(shape))
    return pl.pallas_call(
        _conv_kernel,
        out_shape=jax.ShapeDtypeStruct((b, s, c), BF16),
        grid=(b, s // tm),
        in_specs=[pl.BlockSpec((1, tm, c), lambda bi, i: (bi, i, 0)),
                  const(wdw.shape), const(bdw.shape), const(gng.shape), const(gnb.shape),
                  const(gout.shape), const(gmat.shape)],
        out_specs=pl.BlockSpec((1, tm, c), lambda bi, i: (bi, i, 0)),
        scratch_shapes=[pltpu.VMEM((HALO + tm, c), F32)],
        compiler_params=pltpu.CompilerParams(
            dimension_semantics=("arbitrary", "arbitrary"), vmem_limit_bytes=VMEM_LIMIT),
        name="conv",
    )(uc, wdw, bdw, gng, gnb, gout, gmat)


def _attn_kernel(qt_ref, aqt_ref, k_ref, ak_ref, vt_ref, o_ref, s_scr, p_scr):
    qi = pl.program_id(2)
    tq = qt_ref.shape[2]
    assert tq == TK, "the drain handles exactly one diagonal key block"

    zeros_half = jnp.zeros((HEAD_DIM, tq), BF16)
    zeros_tail = jnp.zeros((LANES - AUG_ROWS, tq), BF16)
    qcat_t = []
    for j in range(2):
        qh = qt_ref[0, HEAD_DIM * j:HEAD_DIM * (j + 1), :]
        halves = [qh, zeros_half] if j == 0 else [zeros_half, qh]
        qcat_t.append(jnp.concatenate(
            halves + [aqt_ref[0, AUG_ROWS * j:AUG_ROWS * (j + 1), :], zeros_tail], axis=0))

    def logits(kb, j):
        start = pl.multiple_of(kb * TK, TK)
        kcat = jnp.concatenate([k_ref[0, pl.ds(start, TK), :], ak_ref[0, pl.ds(start, TK), :]],
                               axis=-1)
        return _dot(kcat, qcat_t[j])

    def softmax_update(s, m, l):
        m_new = jnp.maximum(m, jnp.max(s, axis=0, keepdims=True))
        alpha = jnp.exp(m - m_new)
        p = jnp.exp(s - m_new)
        return p.astype(BF16), alpha, m_new, alpha * l + jnp.sum(p, axis=0, keepdims=True)

    def accumulate(kb, j, p, alpha, acc):
        vt = vt_ref[0, kb, HEAD_DIM * j:HEAD_DIM * (j + 1), :]
        return alpha * acc + _dot(vt, p)

    for j in range(2):
        s_scr[j] = logits(0, j)
        p_scr[j] = jnp.zeros((TK, tq), BF16)

    def body(kb, carry):
        out = []
        for j in range(2):
            m, l, acc, alpha_prev = carry[j]
            s_next = logits(kb + 1, j)
            p, alpha, m, l = softmax_update(s_scr[j], m, l)
            acc = accumulate(jnp.maximum(kb - 1, 0), j, p_scr[j], alpha_prev, acc)
            s_scr[j] = s_next
            p_scr[j] = p
            out.append((m, l, acc, alpha))
        return tuple(out)

    init = tuple((jnp.full((1, tq), NEG_BIG, F32), jnp.zeros((1, tq), F32),
                  jnp.zeros((HEAD_DIM, tq), F32), jnp.ones((1, tq), F32)) for _ in range(2))
    carry = lax.fori_loop(0, qi, body, init)

    key = lax.broadcasted_iota(jnp.int32, (TK, tq), 0)
    qry = lax.broadcasted_iota(jnp.int32, (TK, tq), 1)
    outs = []
    for j in range(2):
        m, l, acc, alpha_prev = carry[j]
        s = jnp.where(key <= qry, s_scr[j], NEG_BIG)
        p, alpha, m, l = softmax_update(s, m, l)
        acc = accumulate(jnp.maximum(qi - 1, 0), j, p_scr[j], alpha_prev, acc)
        acc = accumulate(qi, j, p, alpha, acc)
        outs.append(acc / l)
    o_ref[0] = jnp.concatenate(outs, axis=0).T


def _attn(qt, augq_t, k, augk, vt):
    b, s, _ = k.shape
    return pl.pallas_call(
        _attn_kernel,
        out_shape=jax.ShapeDtypeStruct((b, s, ATTN_WIDTH), F32),
        grid=(b, N_PAIRS, s // TQ),
        in_specs=[pl.BlockSpec((1, LANES, TQ), lambda bi, p, i: (bi, p, i)),
                  pl.BlockSpec((1, 2 * AUG_ROWS, TQ), lambda bi, p, i: (bi, p, i)),
                  pl.BlockSpec((1, s, LANES), lambda bi, p, i: (bi, 0, p)),
                  pl.BlockSpec((1, s, LANES), lambda bi, p, i: (bi, 0, p)),
                  pl.BlockSpec((1, s // TK, LANES, TK), lambda bi, p, i: (bi, 0, p, 0))],
        out_specs=pl.BlockSpec((1, TQ, LANES), lambda bi, p, i: (bi, i, p)),
        scratch_shapes=[pltpu.VMEM((2, TK, TQ), F32), pltpu.VMEM((2, TK, TQ), BF16)],
        compiler_params=pltpu.CompilerParams(
            dimension_semantics=("arbitrary", "arbitrary", "arbitrary"),
            vmem_limit_bytes=VMEM_LIMIT),
        name="attn",
    )(qt, augq_t, k, augk, vt)


def _layernorm(y, g, b):
    mu = jnp.mean(y, axis=-1, keepdims=True)
    d = y - mu
    var = jnp.mean(d * d, axis=-1, keepdims=True)
    return d * lax.rsqrt(var + LN_EPS) * g + b


def _outproj_kernel(attn_ref, cn_ref, x_ref, mod_ref, gattn_ref, wo_a_ref, wo_c_ref,
                    lng_ref, lnb_ref, o_ref):
    a = attn_ref[0]
    ms = jnp.mean(a * a, axis=-1, keepdims=True)
    an = (a * lax.rsqrt(ms + LN_EPS) * gattn_ref[...]).astype(BF16)
    mixed = _dot(an, wo_a_ref[...]) + _dot(cn_ref[0], wo_c_ref[...])
    gate = mod_ref[0, 2:3, :]
    y = DEEPNORM_ALPHA * x_ref[0] + (1.0 + gate) * mixed
    o_ref[0] = _layernorm(y, lng_ref[...], lnb_ref[...])


def _outproj(attn, cn, x, mod1, gattn, wo_a, wo_c, lng, lnb):
    b, s, d = x.shape
    tm = TM_PROJ
    const = lambda shape: pl.BlockSpec(shape, lambda bi, i: (0,) * len(shape))
    return pl.pallas_call(
        _outproj_kernel,
        out_shape=jax.ShapeDtypeStruct((b, s, d), F32),
        grid=(b, s // tm),
        in_specs=[pl.BlockSpec((1, tm, ATTN_WIDTH), lambda bi, i: (bi, i, 0)),
                  pl.BlockSpec((1, tm, CONV_WIDTH), lambda bi, i: (bi, i, 0)),
                  pl.BlockSpec((1, tm, d), lambda bi, i: (bi, i, 0)),
                  pl.BlockSpec((1, 3, d), lambda bi, i: (bi, 0, 0)),
                  const(gattn.shape), const(wo_a.shape), const(wo_c.shape),
                  const(lng.shape), const(lnb.shape)],
        out_specs=pl.BlockSpec((1, tm, d), lambda bi, i: (bi, i, 0)),
        compiler_params=pltpu.CompilerParams(
            dimension_semantics=("arbitrary", "arbitrary"), vmem_limit_bytes=VMEM_LIMIT),
        name="outproj",
    )(attn, cn, x, mod1, gattn, wo_a, wo_c, lng, lnb)


def _ffn_kernel(x_ref, mod_ref, w1_ref, w2_ref, lng_ref, lnb_ref, o_ref):
    x = x_ref[0]
    shift = mod_ref[0, 0:1, :]
    scale = mod_ref[0, 1:2, :]
    gate = mod_ref[0, 2:3, :]
    u = (x * (1.0 + scale) + shift).astype(BF16)
    ff = jnp.zeros(x.shape, F32)
    chunk = D_MODEL
    for c in range(D_FF // chunk):
        hid = jnp.maximum(_dot(u, w1_ref[:, c * chunk:(c + 1) * chunk]), 0.0)
        ff = ff + _dot((hid * hid).astype(BF16), w2_ref[c * chunk:(c + 1) * chunk, :])
    y = DEEPNORM_ALPHA * x + (1.0 + gate) * ff
    o_ref[0] = _layernorm(y, lng_ref[...], lnb_ref[...])


def _ffn(x1, mod2, w1, w2, lng, lnb):
    b, s, d = x1.shape
    tm = TM_PROJ
    const = lambda shape: pl.BlockSpec(shape, lambda bi, i: (0,) * len(shape),
                                       pipeline_mode=pl.Buffered(1))
    return pl.pallas_call(
        _ffn_kernel,
        out_shape=jax.ShapeDtypeStruct((b, s, d), F32),
        grid=(b, s // tm),
        in_specs=[pl.BlockSpec((1, tm, d), lambda bi, i: (bi, i, 0)),
                  pl.BlockSpec((1, 3, d), lambda bi, i: (bi, 0, 0)),
                  const(w1.shape), const(w2.shape), const(lng.shape), const(lnb.shape)],
        out_specs=pl.BlockSpec((1, tm, d), lambda bi, i: (bi, i, 0)),
        compiler_params=pltpu.CompilerParams(
            dimension_semantics=("arbitrary", "arbitrary"), vmem_limit_bytes=VMEM_LIMIT),
        name="ffn",
    )(x1, mod2, w1, w2, lng, lnb)


def kernel(x, c, w_ada, b_ada, w_in, b_forget, w_dw, b_dw, gn_g, gn_b, g_attn_out, g_conv_out,
           w_out, ln1_g, ln1_b, w_ff1, w_ff2, ln2_g, ln2_b):
    bsz = x.shape[0]
    layer = 0
    row = lambda v: v.reshape(1, -1)

    c_pad = jnp.pad(c, ((0, 8 - bsz), (0, 0)))
    ada = _ada(c_pad, w_ada[layer], row(b_ada[layer]))[:bsz]
    mods = ada.reshape(bsz, 6, D_MODEL)
    mod1, mod2 = mods[:, 0:3], mods[:, 3:6]

    w = w_in[layer]
    a0 = 3 * ATTN_WIDTH + N_HEADS
    wq, wk, wv = (w[:, i * ATTN_WIDTH:(i + 1) * ATTN_WIDTH] for i in range(3))
    wqv_t = jnp.concatenate([wq, wv], axis=1).T.astype(BF16)
    wk = wk.astype(BF16)
    reps = LANES // N_HEADS
    wf = jnp.tile(w[:, 3 * ATTN_WIDTH:a0], (1, reps)).astype(BF16)
    bf = jnp.tile(b_forget[layer], reps).reshape(1, LANES)
    wa = w[:, a0:a0 + CONV_WIDTH].astype(BF16)
    wg = w[:, a0 + CONV_WIDTH:].astype(BF16)

    pselq_t, pselk = _decay_routing_matrices()
    qt, vt, k, augq_t, augk, uc = _inproj(x, mod1, wqv_t, wk, wf, bf, wa, wg, pselq_t, pselk)

    grp = np.arange(CONV_WIDTH) // CONV_GROUP
    gmat = jnp.asarray((grp[:, None] == grp[None, :]).astype(np.float32) / CONV_GROUP, BF16)
    cn = _conv(uc, w_dw[layer].reshape(CONV_KERNEL, CONV_WIDTH), row(b_dw[layer]),
               row(gn_g[layer]), row(gn_b[layer]), row(g_conv_out[layer]), gmat)

    attn = _attn(qt, augq_t, k, augk, vt)

    wo = w_out[layer].astype(BF16)
    x1 = _outproj(attn, cn, x, mod1, row(g_attn_out[layer]), wo[:ATTN_WIDTH], wo[ATTN_WIDTH:],
                  row(ln1_g[layer]), row(ln1_b[layer]))

    return _ffn(x1, mod2, w_ff1[layer].astype(BF16), w_ff2[layer].astype(BF16),
                row(ln2_g[layer]), row(ln2_b[layer]))
```

```python
import functools

import numpy as np
import jax
import jax.numpy as jnp
from jax import lax
from jax.experimental import pallas as pl
from jax.experimental.pallas import tpu as pltpu

D_MODEL = 1024
HEAD_DIM = 64
ATTN_WIDTH = 512
CONV_WIDTH = 512
N_HEADS = 8
N_PAIRS = N_HEADS // 2
CONV_KERNEL = 31
CONV_GROUP = 64
D_FF = 4 * D_MODEL
LN_EPS = 1e-5
DEEPNORM_ALPHA = 2.0 ** 0.25

LANES = 128
HALO = 32
AUG_PER_HEAD = 6
AUG_ROWS = 16
NT_DIMS = (((1,), (1,)), ((), ()))
PIECE_LANE = (0, 8, 16)
ONE_LANE = 24
NEG_BIG = -1e30

TM_PROJ = 512
TQ = 256
TK = 256
HEADS_PER_STEP = 4
LOG2E = 1.4426950408889634
VMEM_LIMIT = 56 * 1024 * 1024

F32 = jnp.float32
BF16 = jnp.bfloat16


def _split3(x):
    hi = x.astype(BF16)
    r = x - hi.astype(F32)
    mid = r.astype(BF16)
    lo = (r - mid.astype(F32)).astype(BF16)
    return hi, mid, lo


def _split2(x):
    hi = x.astype(BF16)
    lo = (x - hi.astype(F32)).astype(BF16)
    return hi, lo


def _dot(a, b):
    return jnp.dot(a, b, preferred_element_type=F32)


def _ada_kernel(c_ref, w_ref, b_ref, o_ref):
    c = c_ref[...]
    s = c * jax.nn.sigmoid(c)
    o_ref[...] = jnp.dot(s, w_ref[...], preferred_element_type=F32,
                         precision=lax.Precision.HIGHEST) + b_ref[...]


def _ada(c_pad, w_ada, b_ada):
    n = w_ada.shape[1]
    tn = 1536
    return pl.pallas_call(
        _ada_kernel,
        out_shape=jax.ShapeDtypeStruct((c_pad.shape[0], n), F32),
        grid=(n // tn,),
        in_specs=[pl.BlockSpec(c_pad.shape, lambda j: (0, 0)),
                  pl.BlockSpec((D_MODEL, tn), lambda j: (0, j)),
                  pl.BlockSpec((1, tn), lambda j: (0, j))],
        out_specs=pl.BlockSpec((c_pad.shape[0], tn), lambda j: (0, j)),
        name="ada",
    )(c_pad, w_ada, b_ada)


def _inproj_kernel(x_ref, mod_ref, wqv_t_ref, wk_ref, wf_ref, bf_ref, wa_ref, wg_ref,
                   pselq_t_ref, pselk_ref,
                   qt_ref, vt_ref, k_ref, augq_t_ref, augk_ref, uc_ref, carry_ref):
    tm = x_ref.shape[1]

    @pl.when(pl.program_id(1) == 0)
    def _():
        carry_ref[...] = jnp.zeros_like(carry_ref)

    x = x_ref[0]
    shift = mod_ref[0, 0:1, :]
    scale = mod_ref[0, 1:2, :]
    u = (x * (1.0 + scale) + shift).astype(BF16)

    qv_t = lax.dot_general(wqv_t_ref[...], u, NT_DIMS, preferred_element_type=F32)
    qt_ref[0] = (qv_t[:ATTN_WIDTH] * (LOG2E * HEAD_DIM ** -0.5)).astype(BF16)
    for t in range(tm // TK):
        vt_ref[0, t] = qv_t[ATTN_WIDTH:, t * TK:(t + 1) * TK].astype(BF16)
    k_ref[0] = _dot(u, wk_ref[...]).astype(BF16)

    fl = _dot(u, wf_ref[...]) + bf_ref[...]
    log_f = jnp.minimum(fl, 0.0) - jnp.log(1.0 + jnp.exp(-jnp.abs(fl)))

    row = lax.broadcasted_iota(jnp.int32, (tm, tm), 0)
    col = lax.broadcasted_iota(jnp.int32, (tm, tm), 1)
    tri = jnp.where(row >= col, 1.0, 0.0).astype(BF16)
    h, m, l = _split3(log_f)
    cum = carry_ref[...] + ((_dot(tri, h) + _dot(tri, m)) + _dot(tri, l))
    carry_ref[...] = cum[tm - 1:tm, :]

    lane = lax.broadcasted_iota(jnp.int32, (tm, LANES), 1)
    ch, cm, cl = (t.astype(F32) for t in _split3(cum * LOG2E))
    pieces = jnp.where(lane < PIECE_LANE[1], ch,
                       jnp.where(lane < PIECE_LANE[2], cm,
                                 jnp.where(lane < ONE_LANE, cl,
                                           jnp.where(lane == ONE_LANE, 1.0, 0.0))))
    pieces = pieces.astype(BF16)
    augk_ref[0] = _dot(pieces, pselk_ref[...]).astype(BF16)
    augq_t_ref[0] = lax.dot_general(pselq_t_ref[...], pieces, NT_DIMS,
                                    preferred_element_type=F32).astype(BF16)

    a = _dot(u, wa_ref[...])
    g = _dot(u, wg_ref[...])
    uc_ref[0] = a * jax.nn.sigmoid(g)


def _decay_routing_matrices():
    selq_t = np.zeros((N_HEADS * AUG_ROWS, LANES), np.float32)
    selk = np.zeros((LANES, N_PAIRS * LANES), np.float32)
    for head in range(N_HEADS):
        p, j = divmod(head, 2)
        qbase = head * AUG_ROWS + AUG_PER_HEAD * j
        kbase = p * LANES + AUG_PER_HEAD * j
        for i in range(3):
            selq_t[qbase + i, PIECE_LANE[i] + head] = 1.0
            selq_t[qbase + 3 + i, ONE_LANE] = 1.0
            selk[ONE_LANE, kbase + i] = 1.0
            selk[PIECE_LANE[i] + head, kbase + 3 + i] = -1.0
    return jnp.asarray(selq_t, BF16), jnp.asarray(selk, BF16)


def _inproj(x, mod1, wqv_t, wk, wf, bf, wa, wg, pselq_t, pselk):
    b, s, d = x.shape
    tm = TM_PROJ
    const = lambda shape: pl.BlockSpec(shape, lambda bi, i: (0,) * len(shape))
    return pl.pallas_call(
        _inproj_kernel,
        out_shape=(jax.ShapeDtypeStruct((b, ATTN_WIDTH, s), BF16),
                   jax.ShapeDtypeStruct((b, s // TK, ATTN_WIDTH, TK), BF16),
                   jax.ShapeDtypeStruct((b, s, ATTN_WIDTH), BF16),
                   jax.ShapeDtypeStruct((b, N_HEADS * AUG_ROWS, s), BF16),
                   jax.ShapeDtypeStruct((b, s, N_PAIRS * LANES), BF16),
                   jax.ShapeDtypeStruct((b, s, CONV_WIDTH), F32)),
        grid=(b, s // tm),
        in_specs=[pl.BlockSpec((1, tm, d), lambda bi, i: (bi, i, 0)),
                  pl.BlockSpec((1, 3, d), lambda bi, i: (bi, 0, 0)),
                  const(wqv_t.shape), const(wk.shape), const(wf.shape), const(bf.shape),
                  const(wa.shape), const(wg.shape), const(pselq_t.shape), const(pselk.shape)],
        out_specs=(pl.BlockSpec((1, ATTN_WIDTH, tm), lambda bi, i: (bi, 0, i)),
                   pl.BlockSpec((1, tm // TK, ATTN_WIDTH, TK), lambda bi, i: (bi, i, 0, 0)),
                   pl.BlockSpec((1, tm, ATTN_WIDTH), lambda bi, i: (bi, i, 0)),
                   pl.BlockSpec((1, N_HEADS * AUG_ROWS, tm), lambda bi, i: (bi, 0, i)),
                   pl.BlockSpec((1, tm, N_PAIRS * LANES), lambda bi, i: (bi, i, 0)),
                   pl.BlockSpec((1, tm, CONV_WIDTH), lambda bi, i: (bi, i, 0))),
        scratch_shapes=[pltpu.VMEM((1, LANES), F32)],
        compiler_params=pltpu.CompilerParams(
            dimension_semantics=("arbitrary", "arbitrary"), vmem_limit_bytes=VMEM_LIMIT),
        name="inproj",
    )(x, mod1, wqv_t, wk, wf, bf, wa, wg, pselq_t, pselk)


def _conv_kernel(uc_ref, wdw_ref, bdw_ref, gng_ref, gnb_ref, gout_ref, gmat_ref,
                 o_ref, ext_ref):
    tm = uc_ref.shape[1]

    @pl.when(pl.program_id(1) == 0)
    def _():
        ext_ref[0:HALO, :] = jnp.zeros((HALO, CONV_WIDTH), F32)

    ext_ref[HALO:, :] = uc_ref[0]
    first = HALO - (CONV_KERNEL - 1)
    y = jnp.zeros((tm, CONV_WIDTH), F32) + bdw_ref[...]
    for k in range(CONV_KERNEL):
        y = y + wdw_ref[k:k + 1, :] * ext_ref[first + k:first + k + tm, :]
    ext_ref[0:HALO, :] = ext_ref[tm:tm + HALO, :]

    gmat = gmat_ref[...]
    yh, yl = _split2(y)
    mu = _dot(yh, gmat) + _dot(yl, gmat)
    d = y - mu
    dh, dl = _split2(d * d)
    var = _dot(dh, gmat) + _dot(dl, gmat)
    yn = d * lax.rsqrt(var + LN_EPS) * gng_ref[...] + gnb_ref[...]
    sw = yn * jax.nn.sigmoid(yn)
    ms = jnp.mean(sw * sw, axis=-1, keepdims=True)
    o_ref[0] = (sw * lax.rsqrt(ms + LN_EPS) * gout_ref[...]).astype(BF16)


def _conv(uc, wdw, bdw, gng, gnb, gout, gmat):
    b, s, c = uc.shape
    tm = TM_PROJ
    const = lambda shape: pl.BlockSpec(shape, lambda bi, i: (0,) * len(shape))
    return pl.pallas_call(
        _conv_kernel,
        out_shape=jax.ShapeDtypeStruct((b, s, c), BF16),
        grid=(b, s // tm),
        in_specs=[pl.BlockSpec((1, tm, c), lambda bi, i: (bi, i, 0)),
                  const(wdw.shape), const(bdw.shape), const(gng.shape), const(gnb.shape),
                  const(gout.shape), const(gmat.shape)],
        out_specs=pl.BlockSpec((1, tm, c), lambda bi, i: (bi, i, 0)),
        scratch_shapes=[pltpu.VMEM((HALO + tm, c), F32)],
        compiler_params=pltpu.CompilerParams(
            dimension_semantics=("arbitrary", "arbitrary"), vmem_limit_bytes=VMEM_LIMIT),
        name="conv",
    )(uc, wdw, bdw, gng, gnb, gout, gmat)


def _attn_kernel(qt_ref, aqt_ref, k_ref, ak_ref, vt_ref, o_ref, s_scr, p_scr):
    qi = pl.program_id(2)
    tq = qt_ref.shape[2]
    assert tq == TK, "the drain handles exactly one diagonal key block"
    heads = range(HEADS_PER_STEP)

    zeros_half = jnp.zeros((HEAD_DIM, tq), BF16)
    zeros_tail = jnp.zeros((LANES - AUG_ROWS, tq), BF16)
    qcat_t = []
    for h in heads:
        qh = qt_ref[0, HEAD_DIM * h:HEAD_DIM * (h + 1), :]
        halves = [qh, zeros_half] if h % 2 == 0 else [zeros_half, qh]
        qcat_t.append(jnp.concatenate(
            halves + [aqt_ref[0, AUG_ROWS * h:AUG_ROWS * (h + 1), :], zeros_tail], axis=0))

    def logits(kb, h):
        start = pl.multiple_of(kb * TK, TK)
        pair = pl.ds((h // 2) * LANES, LANES)
        kcat = jnp.concatenate([k_ref[0, pl.ds(start, TK), pair], ak_ref[0, pl.ds(start, TK), pair]],
                               axis=-1)
        return _dot(kcat, qcat_t[h])

    def softmax_update(s, m, l):
        m_new = jnp.maximum(m, jnp.max(s, axis=0, keepdims=True))
        alpha = jnp.exp2(m - m_new)
        p = jnp.exp2(s - m_new)
        return p.astype(BF16), alpha, m_new, alpha * l + jnp.sum(p, axis=0, keepdims=True)

    def accumulate(kb, h, p, alpha, acc):
        vt = vt_ref[0, kb, HEAD_DIM * h:HEAD_DIM * (h + 1), :]
        return alpha * acc + _dot(vt, p)

    for h in heads:
        s_scr[h] = logits(0, h)
        p_scr[h] = jnp.zeros((TK, tq), BF16)

    def body(kb, carry):
        out = []
        for h in heads:
            m, l, acc, alpha_prev = carry[h]
            s_next = logits(kb + 1, h)
            p, alpha, m, l = softmax_update(s_scr[h], m, l)
            acc = accumulate(jnp.maximum(kb - 1, 0), h, p_scr[h], alpha_prev, acc)
            s_scr[h] = s_next
            p_scr[h] = p
            out.append((m, l, acc, alpha))
        return tuple(out)

    init = tuple((jnp.full((1, tq), NEG_BIG, F32), jnp.zeros((1, tq), F32),
                  jnp.zeros((HEAD_DIM, tq), F32), jnp.ones((1, tq), F32)) for _ in heads)
    carry = lax.fori_loop(0, qi, body, init)

    key = lax.broadcasted_iota(jnp.int32, (TK, tq), 0)
    qry = lax.broadcasted_iota(jnp.int32, (TK, tq), 1)
    outs = []
    for h in heads:
        m, l, acc, alpha_prev = carry[h]
        s = jnp.where(key <= qry, s_scr[h], NEG_BIG)
        p, alpha, m, l = softmax_update(s, m, l)
        acc = accumulate(jnp.maximum(qi - 1, 0), h, p_scr[h], alpha_prev, acc)
        acc = accumulate(qi, h, p, alpha, acc)
        outs.append(acc / l)
    for pr in range(HEADS_PER_STEP // 2):
        o_ref[0, :, pr * LANES:(pr + 1) * LANES] = jnp.concatenate(outs[2 * pr:2 * pr + 2], axis=0).T


def _attn(qt, augq_t, k, augk, vt):
    b, s, _ = k.shape
    g = HEADS_PER_STEP
    return pl.pallas_call(
        _attn_kernel,
        out_shape=jax.ShapeDtypeStruct((b, s, ATTN_WIDTH), F32),
        grid=(b, N_HEADS // g, s // TQ),
        in_specs=[pl.BlockSpec((1, g * HEAD_DIM, TQ), lambda bi, p, i: (bi, p, i)),
                  pl.BlockSpec((1, g * AUG_ROWS, TQ), lambda bi, p, i: (bi, p, i)),
                  pl.BlockSpec((1, s, g * HEAD_DIM), lambda bi, p, i: (bi, 0, p)),
                  pl.BlockSpec((1, s, g * HEAD_DIM), lambda bi, p, i: (bi, 0, p)),
                  pl.BlockSpec((1, s // TK, g * HEAD_DIM, TK), lambda bi, p, i: (bi, 0, p, 0))],
        out_specs=pl.BlockSpec((1, TQ, g * HEAD_DIM), lambda bi, p, i: (bi, i, p)),
        scratch_shapes=[pltpu.VMEM((g, TK, TQ), F32), pltpu.VMEM((g, TK, TQ), BF16)],
        compiler_params=pltpu.CompilerParams(
            dimension_semantics=("arbitrary", "arbitrary", "arbitrary"),
            vmem_limit_bytes=VMEM_LIMIT),
        name="attn",
    )(qt, augq_t, k, augk, vt)


def _layernorm(y, g, b):
    mu = jnp.mean(y, axis=-1, keepdims=True)
    d = y - mu
    var = jnp.mean(d * d, axis=-1, keepdims=True)
    return d * lax.rsqrt(var + LN_EPS) * g + b


def _outproj_kernel(attn_ref, cn_ref, x_ref, mod_ref, gattn_ref, wo_a_ref, wo_c_ref,
                    lng_ref, lnb_ref, o_ref):
    a = attn_ref[0]
    ms = jnp.mean(a * a, axis=-1, keepdims=True)
    an = (a * lax.rsqrt(ms + LN_EPS) * gattn_ref[...]).astype(BF16)
    mixed = _dot(an, wo_a_ref[...]) + _dot(cn_ref[0], wo_c_ref[...])
    gate = mod_ref[0, 2:3, :]
    y = DEEPNORM_ALPHA * x_ref[0] + (1.0 + gate) * mixed
    o_ref[0] = _layernorm(y, lng_ref[...], lnb_ref[...])


def _outproj(attn, cn, x, mod1, gattn, wo_a, wo_c, lng, lnb):
    b, s, d = x.shape
    tm = TM_PROJ
    const = lambda shape: pl.BlockSpec(shape, lambda bi, i: (0,) * len(shape))
    return pl.pallas_call(
        _outproj_kernel,
        out_shape=jax.ShapeDtypeStruct((b, s, d), F32),
        grid=(b, s // tm),
        in_specs=[pl.BlockSpec((1, tm, ATTN_WIDTH), lambda bi, i: (bi, i, 0)),
                  pl.BlockSpec((1, tm, CONV_WIDTH), lambda bi, i: (bi, i, 0)),
                  pl.BlockSpec((1, tm, d), lambda bi, i: (bi, i, 0)),
                  pl.BlockSpec((1, 3, d), lambda bi, i: (bi, 0, 0)),
                  const(gattn.shape), const(wo_a.shape), const(wo_c.shape),
                  const(lng.shape), const(lnb.shape)],
        out_specs=pl.BlockSpec((1, tm, d), lambda bi, i: (bi, i, 0)),
        compiler_params=pltpu.CompilerParams(
            dimension_semantics=("arbitrary", "arbitrary"), vmem_limit_bytes=VMEM_LIMIT),
        name="outproj",
    )(attn, cn, x, mod1, gattn, wo_a, wo_c, lng, lnb)


def _ffn_kernel(x_ref, mod_ref, w1_ref, w2_ref, lng_ref, lnb_ref, o_ref):
    x = x_ref[0]
    shift = mod_ref[0, 0:1, :]
    scale = mod_ref[0, 1:2, :]
    gate = mod_ref[0, 2:3, :]
    u = (x * (1.0 + scale) + shift).astype(BF16)
    ff = jnp.zeros(x.shape, F32)
    chunk = D_MODEL
    for c in range(D_FF // chunk):
        hid = jnp.maximum(_dot(u, w1_ref[:, c * chunk:(c + 1) * chunk]), 0.0)
        ff = ff + _dot((hid * hid).astype(BF16), w2_ref[c * chunk:(c + 1) * chunk, :])
    y = DEEPNORM_ALPHA * x + (1.0 + gate) * ff
    o_ref[0] = _layernorm(y, lng_ref[...], lnb_ref[...])


def _ffn(x1, mod2, w1, w2, lng, lnb):
    b, s, d = x1.shape
    tm = TM_PROJ
    const = lambda shape: pl.BlockSpec(shape, lambda bi, i: (0,) * len(shape),
                                       pipeline_mode=pl.Buffered(1))
    return pl.pallas_call(
        _ffn_kernel,
        out_shape=jax.ShapeDtypeStruct((b, s, d), F32),
        grid=(b, s // tm),
        in_specs=[pl.BlockSpec((1, tm, d), lambda bi, i: (bi, i, 0)),
                  pl.BlockSpec((1, 3, d), lambda bi, i: (bi, 0, 0)),
                  const(w1.shape), const(w2.shape), const(lng.shape), const(lnb.shape)],
        out_specs=pl.BlockSpec((1, tm, d), lambda bi, i: (bi, i, 0)),
        compiler_params=pltpu.CompilerParams(
            dimension_semantics=("arbitrary", "arbitrary"), vmem_limit_bytes=VMEM_LIMIT),
        name="ffn",
    )(x1, mod2, w1, w2, lng, lnb)


def kernel(x, c, w_ada, b_ada, w_in, b_forget, w_dw, b_dw, gn_g, gn_b, g_attn_out, g_conv_out,
           w_out, ln1_g, ln1_b, w_ff1, w_ff2, ln2_g, ln2_b):
    bsz = x.shape[0]
    layer = 0
    row = lambda v: v.reshape(1, -1)

    c_pad = jnp.pad(c, ((0, 8 - bsz), (0, 0)))
    ada = _ada(c_pad, w_ada[layer], row(b_ada[layer]))[:bsz]
    mods = ada.reshape(bsz, 6, D_MODEL)
    mod1, mod2 = mods[:, 0:3], mods[:, 3:6]

    w = w_in[layer]
    a0 = 3 * ATTN_WIDTH + N_HEADS
    wq, wk, wv = (w[:, i * ATTN_WIDTH:(i + 1) * ATTN_WIDTH] for i in range(3))
    wqv_t = jnp.concatenate([wq, wv], axis=1).T.astype(BF16)
    wk = wk.astype(BF16)
    reps = LANES // N_HEADS
    wf = jnp.tile(w[:, 3 * ATTN_WIDTH:a0], (1, reps)).astype(BF16)
    bf = jnp.tile(b_forget[layer], reps).reshape(1, LANES)
    wa = w[:, a0:a0 + CONV_WIDTH].astype(BF16)
    wg = w[:, a0 + CONV_WIDTH:].astype(BF16)

    pselq_t, pselk = _decay_routing_matrices()
    qt, vt, k, augq_t, augk, uc = _inproj(x, mod1, wqv_t, wk, wf, bf, wa, wg, pselq_t, pselk)

    grp = np.arange(CONV_WIDTH) // CONV_GROUP
    gmat = jnp.asarray((grp[:, None] == grp[None, :]).astype(np.float32) / CONV_GROUP, BF16)
    cn = _conv(uc, w_dw[layer].reshape(CONV_KERNEL, CONV_WIDTH), row(b_dw[layer]),
               row(gn_g[layer]), row(gn_b[layer]), row(g_conv_out[layer]), gmat)

    attn = _attn(qt, augq_t, k, augk, vt)

    wo = w_out[layer].astype(BF16)
    x1 = _outproj(attn, cn, x, mod1, row(g_attn_out[layer]), wo[:ATTN_WIDTH], wo[ATTN_WIDTH:],
                  row(ln1_g[layer]), row(ln1_b[layer]))

    return _ffn(x1, mod2, w_ff1[layer].astype(BF16), w_ff2[layer].astype(BF16),
                row(ln2_g[layer]), row(ln2_b[layer]))
```

```python
import functools

import numpy as np
import jax
import jax.numpy as jnp
from jax import lax
from jax.experimental import pallas as pl
from jax.experimental.pallas import tpu as pltpu

D_MODEL = 1024
HEAD_DIM = 64
ATTN_WIDTH = 512
CONV_WIDTH = 512
N_HEADS = 8
N_PAIRS = N_HEADS // 2
CONV_KERNEL = 31
CONV_GROUP = 64
D_FF = 4 * D_MODEL
LN_EPS = 1e-5
DEEPNORM_ALPHA = 2.0 ** 0.25

LANES = 128
HALO = 32
AUG_PER_HEAD = 6
AUG_ROWS = 16
NT_DIMS = (((1,), (1,)), ((), ()))
PIECE_LANE = (0, 8, 16)
ONE_LANE = 24
NEG_BIG = -1e30

TM_PROJ = 512
TQ = 256
TK = 256
HEADS_PER_STEP = 8
LOG2E = 1.4426950408889634
VMEM_LIMIT = 56 * 1024 * 1024

F32 = jnp.float32
BF16 = jnp.bfloat16


def _split3(x):
    hi = x.astype(BF16)
    r = x - hi.astype(F32)
    mid = r.astype(BF16)
    lo = (r - mid.astype(F32)).astype(BF16)
    return hi, mid, lo


def _split2(x):
    hi = x.astype(BF16)
    lo = (x - hi.astype(F32)).astype(BF16)
    return hi, lo


def _dot(a, b):
    return jnp.dot(a, b, preferred_element_type=F32)


def _ada_kernel(c_ref, w_ref, b_ref, o_ref):
    c = c_ref[...]
    s = c * jax.nn.sigmoid(c)
    o_ref[...] = jnp.dot(s, w_ref[...], preferred_element_type=F32,
                         precision=lax.Precision.HIGHEST) + b_ref[...]


def _ada(c_pad, w_ada, b_ada):
    n = w_ada.shape[1]
    tn = 1536
    return pl.pallas_call(
        _ada_kernel,
        out_shape=jax.ShapeDtypeStruct((c_pad.shape[0], n), F32),
        grid=(n // tn,),
        in_specs=[pl.BlockSpec(c_pad.shape, lambda j: (0, 0)),
                  pl.BlockSpec((D_MODEL, tn), lambda j: (0, j)),
                  pl.BlockSpec((1, tn), lambda j: (0, j))],
        out_specs=pl.BlockSpec((c_pad.shape[0], tn), lambda j: (0, j)),
        name="ada",
    )(c_pad, w_ada, b_ada)


def _inproj_kernel(x_ref, mod_ref, wqv_t_ref, wk_ref, wf_ref, bf_ref, wa_ref, wg_ref,
                   pselq_t_ref, pselk_ref,
                   qt_ref, vt_ref, k_ref, augq_t_ref, augk_ref, uc_ref, carry_ref):
    tm = x_ref.shape[1]

    @pl.when(pl.program_id(1) == 0)
    def _():
        carry_ref[...] = jnp.zeros_like(carry_ref)

    x = x_ref[0]
    shift = mod_ref[0, 0:1, :]
    scale = mod_ref[0, 1:2, :]
    u = (x * (1.0 + scale) + shift).astype(BF16)

    qv_t = lax.dot_general(wqv_t_ref[...], u, NT_DIMS, preferred_element_type=F32)
    qt_ref[0] = (qv_t[:ATTN_WIDTH] * (LOG2E * HEAD_DIM ** -0.5)).astype(BF16)
    for t in range(tm // TK):
        vt_ref[0, t] = qv_t[ATTN_WIDTH:, t * TK:(t + 1) * TK].astype(BF16)
    k_ref[0] = _dot(u, wk_ref[...]).astype(BF16)

    fl = _dot(u, wf_ref[...]) + bf_ref[...]
    log_f = jnp.minimum(fl, 0.0) - jnp.log(1.0 + jnp.exp(-jnp.abs(fl)))

    row = lax.broadcasted_iota(jnp.int32, (tm, tm), 0)
    col = lax.broadcasted_iota(jnp.int32, (tm, tm), 1)
    tri = jnp.where(row >= col, 1.0, 0.0).astype(BF16)
    h, m, l = _split3(log_f)
    cum = carry_ref[...] + ((_dot(tri, h) + _dot(tri, m)) + _dot(tri, l))
    carry_ref[...] = cum[tm - 1:tm, :]

    lane = lax.broadcasted_iota(jnp.int32, (tm, LANES), 1)
    ch, cm, cl = (t.astype(F32) for t in _split3(cum * LOG2E))
    pieces = jnp.where(lane < PIECE_LANE[1], ch,
                       jnp.where(lane < PIECE_LANE[2], cm,
                                 jnp.where(lane < ONE_LANE, cl,
                                           jnp.where(lane == ONE_LANE, 1.0, 0.0))))
    pieces = pieces.astype(BF16)
    augk_ref[0] = _dot(pieces, pselk_ref[...]).astype(BF16)
    augq_t_ref[0] = lax.dot_general(pselq_t_ref[...], pieces, NT_DIMS,
                                    preferred_element_type=F32).astype(BF16)

    a = _dot(u, wa_ref[...])
    g = _dot(u, wg_ref[...])
    uc_ref[0] = a * jax.nn.sigmoid(g)


def _decay_routing_matrices():
    selq_t = np.zeros((N_HEADS * AUG_ROWS, LANES), np.float32)
    selk = np.zeros((LANES, N_PAIRS * LANES), np.float32)
    for head in range(N_HEADS):
        p, j = divmod(head, 2)
        qbase = head * AUG_ROWS + AUG_PER_HEAD * j
        kbase = p * LANES + AUG_PER_HEAD * j
        for i in range(3):
            selq_t[qbase + i, PIECE_LANE[i] + head] = 1.0
            selq_t[qbase + 3 + i, ONE_LANE] = 1.0
            selk[ONE_LANE, kbase + i] = 1.0
            selk[PIECE_LANE[i] + head, kbase + 3 + i] = -1.0
    return jnp.asarray(selq_t, BF16), jnp.asarray(selk, BF16)


def _inproj(x, mod1, wqv_t, wk, wf, bf, wa, wg, pselq_t, pselk):
    b, s, d = x.shape
    tm = TM_PROJ
    const = lambda shape: pl.BlockSpec(shape, lambda bi, i: (0,) * len(shape))
    return pl.pallas_call(
        _inproj_kernel,
        out_shape=(jax.ShapeDtypeStruct((b, ATTN_WIDTH, s), BF16),
                   jax.ShapeDtypeStruct((b, s // TK, ATTN_WIDTH, TK), BF16),
                   jax.ShapeDtypeStruct((b, s, ATTN_WIDTH), BF16),
                   jax.ShapeDtypeStruct((b, N_HEADS * AUG_ROWS, s), BF16),
                   jax.ShapeDtypeStruct((b, s, N_PAIRS * LANES), BF16),
                   jax.ShapeDtypeStruct((b, s, CONV_WIDTH), F32)),
        grid=(b, s // tm),
        in_specs=[pl.BlockSpec((1, tm, d), lambda bi, i: (bi, i, 0)),
                  pl.BlockSpec((1, 3, d), lambda bi, i: (bi, 0, 0)),
                  const(wqv_t.shape), const(wk.shape), const(wf.shape), const(bf.shape),
                  const(wa.shape), const(wg.shape), const(pselq_t.shape), const(pselk.shape)],
        out_specs=(pl.BlockSpec((1, ATTN_WIDTH, tm), lambda bi, i: (bi, 0, i)),
                   pl.BlockSpec((1, tm // TK, ATTN_WIDTH, TK), lambda bi, i: (bi, i, 0, 0)),
                   pl.BlockSpec((1, tm, ATTN_WIDTH), lambda bi, i: (bi, i, 0)),
                   pl.BlockSpec((1, N_HEADS * AUG_ROWS, tm), lambda bi, i: (bi, 0, i)),
                   pl.BlockSpec((1, tm, N_PAIRS * LANES), lambda bi, i: (bi, i, 0)),
                   pl.BlockSpec((1, tm, CONV_WIDTH), lambda bi, i: (bi, i, 0))),
        scratch_shapes=[pltpu.VMEM((1, LANES), F32)],
        compiler_params=pltpu.CompilerParams(
            dimension_semantics=("arbitrary", "arbitrary"), vmem_limit_bytes=VMEM_LIMIT),
        name="inproj",
    )(x, mod1, wqv_t, wk, wf, bf, wa, wg, pselq_t, pselk)


def _conv_kernel(uc_ref, wdw_ref, bdw_ref, gng_ref, gnb_ref, gout_ref, gmat_ref,
                 o_ref, ext_ref):
    tm = uc_ref.shape[1]

    @pl.when(pl.program_id(1) == 0)
    def _():
        ext_ref[0:HALO, :] = jnp.zeros((HALO, CONV_WIDTH), F32)

    ext_ref[HALO:, :] = uc_ref[0]
    first = HALO - (CONV_KERNEL - 1)
    y = jnp.zeros((tm, CONV_WIDTH), F32) + bdw_ref[...]
    for k in range(CONV_KERNEL):
        y = y + wdw_ref[k:k + 1, :] * ext_ref[first + k:first + k + tm, :]
    ext_ref[0:HALO, :] = ext_ref[tm:tm + HALO, :]

    gmat = gmat_ref[...]
    yh, yl = _split2(y)
    mu = _dot(yh, gmat) + _dot(yl, gmat)
    d = y - mu
    dh, dl = _split2(d * d)
    var = _dot(dh, gmat) + _dot(dl, gmat)
    yn = d * lax.rsqrt(var + LN_EPS) * gng_ref[...] + gnb_ref[...]
    sw = yn * jax.nn.sigmoid(yn)
    ms = jnp.mean(sw * sw, axis=-1, keepdims=True)
    o_ref[0] = (sw * lax.rsqrt(ms + LN_EPS) * gout_ref[...]).astype(BF16)


def _conv(uc, wdw, bdw, gng, gnb, gout, gmat):
    b, s, c = uc.shape
    tm = TM_PROJ
    const = lambda shape: pl.BlockSpec(shape, lambda bi, i: (0,) * len(shape))
    return pl.pallas_call(
        _conv_kernel,
        out_shape=jax.ShapeDtypeStruct((b, s, c), BF16),
        grid=(b, s // tm),
        in_specs=[pl.BlockSpec((1, tm, c), lambda bi, i: (bi, i, 0)),
                  const(wdw.shape), const(bdw.shape), const(gng.shape), const(gnb.shape),
                  const(gout.shape), const(gmat.shape)],
        out_specs=pl.BlockSpec((1, tm, c), lambda bi, i: (bi, i, 0)),
        scratch_shapes=[pltpu.VMEM((HALO + tm, c), F32)],
        compiler_params=pltpu.CompilerParams(
            dimension_semantics=("arbitrary", "arbitrary"), vmem_limit_bytes=VMEM_LIMIT),
        name="conv",
    )(uc, wdw, bdw, gng, gnb, gout, gmat)


def _attn_kernel(qt_ref, aqt_ref, k_ref, ak_ref, vt_ref, o_ref, s_scr, p_scr):
    qi = pl.program_id(2)
    tq = qt_ref.shape[2]
    assert tq == TK, "the drain handles exactly one diagonal key block"
    heads = range(HEADS_PER_STEP)

    zeros_half = jnp.zeros((HEAD_DIM, tq), BF16)
    zeros_tail = jnp.zeros((LANES - AUG_ROWS, tq), BF16)
    qcat_t = []
    for h in heads:
        qh = qt_ref[0, HEAD_DIM * h:HEAD_DIM * (h + 1), :]
        halves = [qh, zeros_half] if h % 2 == 0 else [zeros_half, qh]
        qcat_t.append(jnp.concatenate(
            halves + [aqt_ref[0, AUG_ROWS * h:AUG_ROWS * (h + 1), :], zeros_tail], axis=0))

    def logits(kb, h):
        start = pl.multiple_of(kb * TK, TK)
        pair = pl.ds((h // 2) * LANES, LANES)
        kcat = jnp.concatenate([k_ref[0, pl.ds(start, TK), pair], ak_ref[0, pl.ds(start, TK), pair]],
                               axis=-1)
        return _dot(kcat, qcat_t[h])

    def block_max(s):
        return jnp.max(s, axis=0, keepdims=True)

    def softmax_update(s, m_blk, m):
        m_new = jnp.maximum(m, m_blk)
        alpha = jnp.exp2(m - m_new)
        p = jnp.exp2((s - m_new).astype(BF16))
        return p, alpha, m_new

    ones_rows = jnp.ones((AUG_ROWS, TK), BF16)

    def accumulate(kb, h, p, alpha, acc):
        vt = jnp.concatenate([vt_ref[0, kb, HEAD_DIM * h:HEAD_DIM * (h + 1), :], ones_rows], axis=0)
        return alpha * acc + _dot(vt, p)

    init = []
    for h in heads:
        s0 = logits(0, h)
        s_scr[h] = s0
        p_scr[h] = jnp.zeros((TK, tq), BF16)
        init.append((jnp.full((1, tq), NEG_BIG, F32), block_max(s0),
                     jnp.zeros((HEAD_DIM + AUG_ROWS, tq), F32), jnp.ones((1, tq), F32)))

    def body(kb, carry):
        out = []
        for h in heads:
            m, m_blk, acc, alpha_prev = carry[h]
            s_next = logits(kb + 1, h)
            p, alpha, m = softmax_update(s_scr[h], m_blk, m)
            acc = accumulate(jnp.maximum(kb - 1, 0), h, p_scr[h], alpha_prev, acc)
            s_scr[h] = s_next
            p_scr[h] = p
            out.append((m, block_max(s_next), acc, alpha))
        return tuple(out)

    carry = lax.fori_loop(0, qi, body, tuple(init))

    key = lax.broadcasted_iota(jnp.int32, (TK, tq), 0)
    qry = lax.broadcasted_iota(jnp.int32, (TK, tq), 1)
    outs = []
    for h in heads:
        m, _, acc, alpha_prev = carry[h]
        s = jnp.where(key <= qry, s_scr[h], NEG_BIG)
        p, alpha, m = softmax_update(s, block_max(s), m)
        acc = accumulate(jnp.maximum(qi - 1, 0), h, p_scr[h], alpha_prev, acc)
        acc = accumulate(qi, h, p, alpha, acc)
        outs.append(acc[:HEAD_DIM] / acc[HEAD_DIM:HEAD_DIM + 1])
    for pr in range(HEADS_PER_STEP // 2):
        o_ref[0, :, pr * LANES:(pr + 1) * LANES] = jnp.concatenate(outs[2 * pr:2 * pr + 2], axis=0).T


def _attn(qt, augq_t, k, augk, vt):
    b, s, _ = k.shape
    g = HEADS_PER_STEP
    return pl.pallas_call(
        _attn_kernel,
        out_shape=jax.ShapeDtypeStruct((b, s, ATTN_WIDTH), F32),
        grid=(b, N_HEADS // g, s // TQ),
        in_specs=[pl.BlockSpec((1, g * HEAD_DIM, TQ), lambda bi, p, i: (bi, p, i)),
                  pl.BlockSpec((1, g * AUG_ROWS, TQ), lambda bi, p, i: (bi, p, i)),
                  pl.BlockSpec((1, s, g * HEAD_DIM), lambda bi, p, i: (bi, 0, p),
                               pipeline_mode=pl.Buffered(1)),
                  pl.BlockSpec((1, s, g * HEAD_DIM), lambda bi, p, i: (bi, 0, p),
                               pipeline_mode=pl.Buffered(1)),
                  pl.BlockSpec((1, s // TK, g * HEAD_DIM, TK), lambda bi, p, i: (bi, 0, p, 0),
                               pipeline_mode=pl.Buffered(1))],
        out_specs=pl.BlockSpec((1, TQ, g * HEAD_DIM), lambda bi, p, i: (bi, i, p)),
        scratch_shapes=[pltpu.VMEM((g, TK, TQ), F32), pltpu.VMEM((g, TK, TQ), BF16)],
        compiler_params=pltpu.CompilerParams(
            dimension_semantics=("arbitrary", "arbitrary", "arbitrary"),
            vmem_limit_bytes=VMEM_LIMIT),
        name="attn",
    )(qt, augq_t, k, augk, vt)


def _layernorm(y, g, b):
    mu = jnp.mean(y, axis=-1, keepdims=True)
    d = y - mu
    var = jnp.mean(d * d, axis=-1, keepdims=True)
    return d * lax.rsqrt(var + LN_EPS) * g + b


def _outproj_kernel(attn_ref, cn_ref, x_ref, mod_ref, gattn_ref, wo_a_ref, wo_c_ref,
                    lng_ref, lnb_ref, o_ref):
    a = attn_ref[0]
    ms = jnp.mean(a * a, axis=-1, keepdims=True)
    an = (a * lax.rsqrt(ms + LN_EPS) * gattn_ref[...]).astype(BF16)
    mixed = _dot(an, wo_a_ref[...]) + _dot(cn_ref[0], wo_c_ref[...])
    gate = mod_ref[0, 2:3, :]
    y = DEEPNORM_ALPHA * x_ref[0] + (1.0 + gate) * mixed
    o_ref[0] = _layernorm(y, lng_ref[...], lnb_ref[...])


def _outproj(attn, cn, x, mod1, gattn, wo_a, wo_c, lng, lnb):
    b, s, d = x.shape
    tm = TM_PROJ
    const = lambda shape: pl.BlockSpec(shape, lambda bi, i: (0,) * len(shape))
    return pl.pallas_call(
        _outproj_kernel,
        out_shape=jax.ShapeDtypeStruct((b, s, d), F32),
        grid=(b, s // tm),
        in_specs=[pl.BlockSpec((1, tm, ATTN_WIDTH), lambda bi, i: (bi, i, 0)),
                  pl.BlockSpec((1, tm, CONV_WIDTH), lambda bi, i: (bi, i, 0)),
                  pl.BlockSpec((1, tm, d), lambda bi, i: (bi, i, 0)),
                  pl.BlockSpec((1, 3, d), lambda bi, i: (bi, 0, 0)),
                  const(gattn.shape), const(wo_a.shape), const(wo_c.shape),
                  const(lng.shape), const(lnb.shape)],
        out_specs=pl.BlockSpec((1, tm, d), lambda bi, i: (bi, i, 0)),
        compiler_params=pltpu.CompilerParams(
            dimension_semantics=("arbitrary", "arbitrary"), vmem_limit_bytes=VMEM_LIMIT),
        name="outproj",
    )(attn, cn, x, mod1, gattn, wo_a, wo_c, lng, lnb)


def _ffn_kernel(x_ref, mod_ref, w1_ref, w2_ref, lng_ref, lnb_ref, o_ref):
    x = x_ref[0]
    shift = mod_ref[0, 0:1, :]
    scale = mod_ref[0, 1:2, :]
    gate = mod_ref[0, 2:3, :]
    u = (x * (1.0 + scale) + shift).astype(BF16)
    ff = jnp.zeros(x.shape, F32)
    chunk = D_MODEL
    for c in range(D_FF // chunk):
        hid = jnp.maximum(_dot(u, w1_ref[:, c * chunk:(c + 1) * chunk]), 0.0)
        ff = ff + _dot((hid * hid).astype(BF16), w2_ref[c * chunk:(c + 1) * chunk, :])
    y = DEEPNORM_ALPHA * x + (1.0 + gate) * ff
    o_ref[0] = _layernorm(y, lng_ref[...], lnb_ref[...])


def _ffn(x1, mod2, w1, w2, lng, lnb):
    b, s, d = x1.shape
    tm = TM_PROJ
    const = lambda shape: pl.BlockSpec(shape, lambda bi, i: (0,) * len(shape),
                                       pipeline_mode=pl.Buffered(1))
    return pl.pallas_call(
        _ffn_kernel,
        out_shape=jax.ShapeDtypeStruct((b, s, d), F32),
        grid=(b, s // tm),
        in_specs=[pl.BlockSpec((1, tm, d), lambda bi, i: (bi, i, 0)),
                  pl.BlockSpec((1, 3, d), lambda bi, i: (bi, 0, 0)),
                  const(w1.shape), const(w2.shape), const(lng.shape), const(lnb.shape)],
        out_specs=pl.BlockSpec((1, tm, d), lambda bi, i: (bi, i, 0)),
        compiler_params=pltpu.CompilerParams(
            dimension_semantics=("arbitrary", "arbitrary"), vmem_limit_bytes=VMEM_LIMIT),
        name="ffn",
    )(x1, mod2, w1, w2, lng, lnb)


def kernel(x, c, w_ada, b_ada, w_in, b_forget, w_dw, b_dw, gn_g, gn_b, g_attn_out, g_conv_out,
           w_out, ln1_g, ln1_b, w_ff1, w_ff2, ln2_g, ln2_b):
    bsz = x.shape[0]
    layer = 0
    row = lambda v: v.reshape(1, -1)

    c_pad = jnp.pad(c, ((0, 8 - bsz), (0, 0)))
    ada = _ada(c_pad, w_ada[layer], row(b_ada[layer]))[:bsz]
    mods = ada.reshape(bsz, 6, D_MODEL)
    mod1, mod2 = mods[:, 0:3], mods[:, 3:6]

    w = w_in[layer]
    a0 = 3 * ATTN_WIDTH + N_HEADS
    wq, wk, wv = (w[:, i * ATTN_WIDTH:(i + 1) * ATTN_WIDTH] for i in range(3))
    wqv_t = jnp.concatenate([wq, wv], axis=1).T.astype(BF16)
    wk = wk.astype(BF16)
    reps = LANES // N_HEADS
    wf = jnp.tile(w[:, 3 * ATTN_WIDTH:a0], (1, reps)).astype(BF16)
    bf = jnp.tile(b_forget[layer], reps).reshape(1, LANES)
    wa = w[:, a0:a0 + CONV_WIDTH].astype(BF16)
    wg = w[:, a0 + CONV_WIDTH:].astype(BF16)

    pselq_t, pselk = _decay_routing_matrices()
    qt, vt, k, augq_t, augk, uc = _inproj(x, mod1, wqv_t, wk, wf, bf, wa, wg, pselq_t, pselk)

    grp = np.arange(CONV_WIDTH) // CONV_GROUP
    gmat = jnp.asarray((grp[:, None] == grp[None, :]).astype(np.float32) / CONV_GROUP, BF16)
    cn = _conv(uc, w_dw[layer].reshape(CONV_KERNEL, CONV_WIDTH), row(b_dw[layer]),
               row(gn_g[layer]), row(gn_b[layer]), row(g_conv_out[layer]), gmat)

    attn = _attn(qt, augq_t, k, augk, vt)

    wo = w_out[layer].astype(BF16)
    x1 = _outproj(attn, cn, x, mod1, row(g_attn_out[layer]), wo[:ATTN_WIDTH], wo[ATTN_WIDTH:],
                  row(ln1_g[layer]), row(ln1_b[layer]))

    return _ffn(x1, mod2, w_ff1[layer].astype(BF16), w_ff2[layer].astype(BF16),
                row(ln2_g[layer]), row(ln2_b[layer]))
```

```python
import functools

import numpy as np
import jax
import jax.numpy as jnp
from jax import lax
from jax.experimental import pallas as pl
from jax.experimental.pallas import tpu as pltpu

D_MODEL = 1024
HEAD_DIM = 64
ATTN_WIDTH = 512
CONV_WIDTH = 512
N_HEADS = 8
N_PAIRS = N_HEADS // 2
CONV_KERNEL = 31
CONV_GROUP = 64
D_FF = 4 * D_MODEL
LN_EPS = 1e-5
DEEPNORM_ALPHA = 2.0 ** 0.25

LANES = 128
SUBLANES = 8
HALO = 32
EXT_TAIL = 8
AUG_PER_HEAD = 6
AUG_ROWS = 16
NT_DIMS = (((1,), (1,)), ((), ()))
PIECE_LANE = (0, 8, 16)
ONE_LANE = 24
NEG_BIG = -1e30

TM_PROJ = 512
TQ = 256
TK = 256
HEADS_PER_STEP = 8
LOG2E = 1.4426950408889634
VMEM_LIMIT = 56 * 1024 * 1024

F32 = jnp.float32
BF16 = jnp.bfloat16


def _split3(x):
    hi = x.astype(BF16)
    r = x - hi.astype(F32)
    mid = r.astype(BF16)
    lo = (r - mid.astype(F32)).astype(BF16)
    return hi, mid, lo


def _split2(x):
    hi = x.astype(BF16)
    lo = (x - hi.astype(F32)).astype(BF16)
    return hi, lo


def _dot(a, b):
    return jnp.dot(a, b, preferred_element_type=F32)


def _ada_kernel(c_ref, w_ref, b_ref, o_ref):
    c = c_ref[...]
    s = c * jax.nn.sigmoid(c)
    o_ref[...] = jnp.dot(s, w_ref[...], preferred_element_type=F32,
                         precision=lax.Precision.HIGHEST) + b_ref[...]


def _ada(c_pad, w_ada, b_ada):
    n = w_ada.shape[1]
    tn = 1536
    return pl.pallas_call(
        _ada_kernel,
        out_shape=jax.ShapeDtypeStruct((c_pad.shape[0], n), F32),
        grid=(n // tn,),
        in_specs=[pl.BlockSpec(c_pad.shape, lambda j: (0, 0)),
                  pl.BlockSpec((D_MODEL, tn), lambda j: (0, j)),
                  pl.BlockSpec((1, tn), lambda j: (0, j))],
        out_specs=pl.BlockSpec((c_pad.shape[0], tn), lambda j: (0, j)),
        name="ada",
    )(c_pad, w_ada, b_ada)


def _inproj_kernel(x_ref, mod_ref, wqv_t_ref, wk_ref, wf_ref, bf_ref, wa_ref, wg_ref,
                   pselq_t_ref, pselk_ref, wdw_ref, bdw_ref, gng_ref, gnb_ref, gout_ref, gmat_ref,
                   qt_ref, vt_ref, k_ref, augq_t_ref, augk_ref, cn_ref, carry_ref, ext_ref):
    tm = x_ref.shape[1]

    @pl.when(pl.program_id(1) == 0)
    def _():
        carry_ref[...] = jnp.zeros_like(carry_ref)
        ext_ref[0:HALO, :] = jnp.zeros((HALO, CONV_WIDTH), F32)
        ext_ref[HALO + tm:, :] = jnp.zeros((EXT_TAIL, CONV_WIDTH), F32)

    x = x_ref[0]
    shift = mod_ref[0, 0:1, :]
    scale = mod_ref[0, 1:2, :]
    u = (x * (1.0 + scale) + shift).astype(BF16)

    ext_ref[HALO:HALO + tm, :] = _dot(u, wa_ref[...]) * jax.nn.sigmoid(_dot(u, wg_ref[...]))

    def qv_rows(lo, hi):
        return lax.dot_general(wqv_t_ref[lo:hi, :], u, NT_DIMS, preferred_element_type=F32)

    def emit_q(lo, hi):
        qt_ref[0, lo:hi, :] = (qv_rows(lo, hi) * (LOG2E * HEAD_DIM ** -0.5)).astype(BF16)

    def emit_v(lo, hi):
        v_t = qv_rows(ATTN_WIDTH + lo, ATTN_WIDTH + hi).astype(BF16)
        for t in range(tm // TK):
            vt_ref[0, t, lo:hi, :] = v_t[:, t * TK:(t + 1) * TK]

    def emit_k():
        k_ref[0] = _dot(u, wk_ref[...]).astype(BF16)

    def emit_decay():
        fl = _dot(u, wf_ref[...]) + bf_ref[...]
        log_f = jnp.minimum(fl, 0.0) - jnp.log(1.0 + jnp.exp(-jnp.abs(fl)))
        row = lax.broadcasted_iota(jnp.int32, (tm, tm), 0)
        col = lax.broadcasted_iota(jnp.int32, (tm, tm), 1)
        tri = jnp.where(row >= col, 1.0, 0.0).astype(BF16)
        h, m, l = _split3(log_f)
        cum = carry_ref[...] + ((_dot(tri, h) + _dot(tri, m)) + _dot(tri, l))
        carry_ref[...] = cum[tm - 1:tm, :]
        lane = lax.broadcasted_iota(jnp.int32, (tm, LANES), 1)
        ch, cm, cl = (t.astype(F32) for t in _split3(cum * LOG2E))
        pieces = jnp.where(lane < PIECE_LANE[1], ch,
                           jnp.where(lane < PIECE_LANE[2], cm,
                                     jnp.where(lane < ONE_LANE, cl,
                                               jnp.where(lane == ONE_LANE, 1.0, 0.0))))
        pieces = pieces.astype(BF16)
        augk_ref[0] = _dot(pieces, pselk_ref[...]).astype(BF16)
        augq_t_ref[0] = lax.dot_general(pselq_t_ref[...], pieces, NT_DIMS,
                                        preferred_element_type=F32).astype(BF16)

    half = ATTN_WIDTH // 2
    matmul_items = [lambda: emit_q(0, half), lambda: emit_q(half, ATTN_WIDTH),
                    lambda: emit_v(0, half), lambda: emit_v(half, ATTN_WIDTH),
                    emit_k, emit_decay]

    first = HALO - (CONV_KERNEL - 1)
    rows = tm + 2 * SUBLANES
    y = jnp.zeros((tm, CONV_WIDTH), F32) + bdw_ref[...]
    for b in range(SUBLANES):
        zb = None
        for a in range(pl.cdiv(CONV_KERNEL - b, SUBLANES)):
            k = SUBLANES * a + b
            term = wdw_ref[k:k + 1, :] * ext_ref[SUBLANES * a:SUBLANES * a + rows, :]
            zb = term if zb is None else zb + term
        y = y + zb[first + b:first + b + tm, :]
        if b < len(matmul_items):
            matmul_items[b]()
    ext_ref[0:HALO, :] = ext_ref[tm:tm + HALO, :]
    cn_ref[0] = _norm_conv_branch(y, gng_ref, gnb_ref, gout_ref, gmat_ref)


def _norm_conv_branch(y, gng_ref, gnb_ref, gout_ref, gmat_ref):
    gmat = gmat_ref[...]
    yh, yl = _split2(y)
    mu = _dot(yh, gmat) + _dot(yl, gmat)
    d = y - mu
    dh, dl = _split2(d * d)
    var = _dot(dh, gmat) + _dot(dl, gmat)
    yn = d * lax.rsqrt(var + LN_EPS) * gng_ref[...] + gnb_ref[...]
    sw = yn * jax.nn.sigmoid(yn)
    ms = jnp.mean(sw * sw, axis=-1, keepdims=True)
    return (sw * lax.rsqrt(ms + LN_EPS) * gout_ref[...]).astype(BF16)


def _decay_routing_matrices():
    selq_t = np.zeros((N_HEADS * AUG_ROWS, LANES), np.float32)
    selk = np.zeros((LANES, N_PAIRS * LANES), np.float32)
    for head in range(N_HEADS):
        p, j = divmod(head, 2)
        qbase = head * AUG_ROWS + AUG_PER_HEAD * j
        kbase = p * LANES + AUG_PER_HEAD * j
        for i in range(3):
            selq_t[qbase + i, PIECE_LANE[i] + head] = 1.0
            selq_t[qbase + 3 + i, ONE_LANE] = 1.0
            selk[ONE_LANE, kbase + i] = 1.0
            selk[PIECE_LANE[i] + head, kbase + 3 + i] = -1.0
    return jnp.asarray(selq_t, BF16), jnp.asarray(selk, BF16)


def _inproj(x, mod1, wqv_t, wk, wf, bf, wa, wg, pselq_t, pselk, *conv_params):
    b, s, d = x.shape
    tm = TM_PROJ
    const = lambda shape: pl.BlockSpec(shape, lambda bi, i: (0,) * len(shape))
    conv_specs = [const(p.shape) for p in conv_params]
    return pl.pallas_call(
        _inproj_kernel,
        out_shape=(jax.ShapeDtypeStruct((b, ATTN_WIDTH, s), BF16),
                   jax.ShapeDtypeStruct((b, s // TK, ATTN_WIDTH, TK), BF16),
                   jax.ShapeDtypeStruct((b, s, ATTN_WIDTH), BF16),
                   jax.ShapeDtypeStruct((b, N_HEADS * AUG_ROWS, s), BF16),
                   jax.ShapeDtypeStruct((b, s, N_PAIRS * LANES), BF16),
                   jax.ShapeDtypeStruct((b, s, CONV_WIDTH), BF16)),
        grid=(b, s // tm),
        in_specs=[pl.BlockSpec((1, tm, d), lambda bi, i: (bi, i, 0)),
                  pl.BlockSpec((1, 3, d), lambda bi, i: (bi, 0, 0)),
                  const(wqv_t.shape), const(wk.shape), const(wf.shape), const(bf.shape),
                  const(wa.shape), const(wg.shape), const(pselq_t.shape), const(pselk.shape)]
                 + conv_specs,
        out_specs=(pl.BlockSpec((1, ATTN_WIDTH, tm), lambda bi, i: (bi, 0, i)),
                   pl.BlockSpec((1, tm // TK, ATTN_WIDTH, TK), lambda bi, i: (bi, i, 0, 0)),
                   pl.BlockSpec((1, tm, ATTN_WIDTH), lambda bi, i: (bi, i, 0)),
                   pl.BlockSpec((1, N_HEADS * AUG_ROWS, tm), lambda bi, i: (bi, 0, i)),
                   pl.BlockSpec((1, tm, N_PAIRS * LANES), lambda bi, i: (bi, i, 0)),
                   pl.BlockSpec((1, tm, CONV_WIDTH), lambda bi, i: (bi, i, 0))),
        scratch_shapes=[pltpu.VMEM((1, LANES), F32),
                        pltpu.VMEM((HALO + tm + EXT_TAIL, CONV_WIDTH), F32)],
        compiler_params=pltpu.CompilerParams(
            dimension_semantics=("arbitrary", "arbitrary"), vmem_limit_bytes=VMEM_LIMIT),
        name="inproj",
    )(x, mod1, wqv_t, wk, wf, bf, wa, wg, pselq_t, pselk, *conv_params)


def _attn_kernel(qt_ref, aqt_ref, k_ref, ak_ref, vt_ref, o_ref, s_scr, p_scr):
    qi = pl.program_id(2)
    tq = qt_ref.shape[2]
    assert tq == TK, "the drain handles exactly one diagonal key block"
    heads = range(HEADS_PER_STEP)

    zeros_half = jnp.zeros((HEAD_DIM, tq), BF16)
    zeros_tail = jnp.zeros((LANES - AUG_ROWS, tq), BF16)
    qcat_t = []
    for h in heads:
        qh = qt_ref[0, HEAD_DIM * h:HEAD_DIM * (h + 1), :]
        halves = [qh, zeros_half] if h % 2 == 0 else [zeros_half, qh]
        qcat_t.append(jnp.concatenate(
            halves + [aqt_ref[0, AUG_ROWS * h:AUG_ROWS * (h + 1), :], zeros_tail], axis=0))

    def logits(kb, h):
        start = pl.multiple_of(kb * TK, TK)
        pair = pl.ds((h // 2) * LANES, LANES)
        kcat = jnp.concatenate([k_ref[0, pl.ds(start, TK), pair], ak_ref[0, pl.ds(start, TK), pair]],
                               axis=-1)
        return _dot(kcat, qcat_t[h])

    def block_max(s):
        return jnp.max(s, axis=0, keepdims=True)

    def softmax_update(s, m_blk, m):
        m_new = jnp.maximum(m, m_blk)
        alpha = jnp.exp2(m - m_new)
        p = jnp.exp2((s - m_new).astype(BF16))
        return p, alpha, m_new

    ones_rows = jnp.ones((AUG_ROWS, TK), BF16)

    def accumulate(kb, h, p, alpha, acc):
        vt = jnp.concatenate([vt_ref[0, kb, HEAD_DIM * h:HEAD_DIM * (h + 1), :], ones_rows], axis=0)
        return alpha * acc + _dot(vt, p)

    init = []
    for h in heads:
        s0 = logits(0, h)
        s_scr[h] = s0
        p_scr[h] = jnp.zeros((TK, tq), BF16)
        init.append((jnp.full((1, tq), NEG_BIG, F32), block_max(s0),
                     jnp.zeros((HEAD_DIM + AUG_ROWS, tq), F32), jnp.ones((1, tq), F32)))

    def body(kb, carry):
        out = []
        for h in heads:
            m, m_blk, acc, alpha_prev = carry[h]
            s_next = logits(kb + 1, h)
            p, alpha, m = softmax_update(s_scr[h], m_blk, m)
            acc = accumulate(jnp.maximum(kb - 1, 0), h, p_scr[h], alpha_prev, acc)
            s_scr[h] = s_next
            p_scr[h] = p
            out.append((m, block_max(s_next), acc, alpha))
        return tuple(out)

    n_pairs = qi // 2
    carry = lax.fori_loop(0, n_pairs, lambda i, c: body(2 * i + 1, body(2 * i, c)), tuple(init))
    carry = lax.fori_loop(2 * n_pairs, qi, body, carry)

    key = lax.broadcasted_iota(jnp.int32, (TK, tq), 0)
    qry = lax.broadcasted_iota(jnp.int32, (TK, tq), 1)
    outs = []
    for h in heads:
        m, _, acc, alpha_prev = carry[h]
        s = jnp.where(key <= qry, s_scr[h], NEG_BIG)
        p, alpha, m = softmax_update(s, block_max(s), m)
        acc = accumulate(jnp.maximum(qi - 1, 0), h, p_scr[h], alpha_prev, acc)
        acc = accumulate(qi, h, p, alpha, acc)
        outs.append(acc[:HEAD_DIM] / acc[HEAD_DIM:HEAD_DIM + 1])
    for pr in range(HEADS_PER_STEP // 2):
        o_ref[0, :, pr * LANES:(pr + 1) * LANES] = jnp.concatenate(outs[2 * pr:2 * pr + 2], axis=0).T


def _attn(qt, augq_t, k, augk, vt):
    b, s, _ = k.shape
    g = HEADS_PER_STEP
    return pl.pallas_call(
        _attn_kernel,
        out_shape=jax.ShapeDtypeStruct((b, s, ATTN_WIDTH), F32),
        grid=(b, N_HEADS // g, s // TQ),
        in_specs=[pl.BlockSpec((1, g * HEAD_DIM, TQ), lambda bi, p, i: (bi, p, i)),
                  pl.BlockSpec((1, g * AUG_ROWS, TQ), lambda bi, p, i: (bi, p, i)),
                  pl.BlockSpec((1, s, g * HEAD_DIM), lambda bi, p, i: (bi, 0, p),
                               pipeline_mode=pl.Buffered(1)),
                  pl.BlockSpec((1, s, g * HEAD_DIM), lambda bi, p, i: (bi, 0, p),
                               pipeline_mode=pl.Buffered(1)),
                  pl.BlockSpec((1, s // TK, g * HEAD_DIM, TK), lambda bi, p, i: (bi, 0, p, 0),
                               pipeline_mode=pl.Buffered(1))],
        out_specs=pl.BlockSpec((1, TQ, g * HEAD_DIM), lambda bi, p, i: (bi, i, p)),
        scratch_shapes=[pltpu.VMEM((g, TK, TQ), F32), pltpu.VMEM((g, TK, TQ), BF16)],
        compiler_params=pltpu.CompilerParams(
            dimension_semantics=("arbitrary", "arbitrary", "arbitrary"),
            vmem_limit_bytes=VMEM_LIMIT),
        name="attn",
    )(qt, augq_t, k, augk, vt)


def _layernorm(y, g, b):
    mu = jnp.mean(y, axis=-1, keepdims=True)
    d = y - mu
    var = jnp.mean(d * d, axis=-1, keepdims=True)
    return d * lax.rsqrt(var + LN_EPS) * g + b


def _outproj_kernel(attn_ref, cn_ref, x_ref, mod_ref, gattn_ref, wo_a_ref, wo_c_ref,
                    lng_ref, lnb_ref, o_ref):
    a = attn_ref[0]
    ms = jnp.mean(a * a, axis=-1, keepdims=True)
    an = (a * lax.rsqrt(ms + LN_EPS) * gattn_ref[...]).astype(BF16)
    mixed = _dot(an, wo_a_ref[...]) + _dot(cn_ref[0], wo_c_ref[...])
    gate = mod_ref[0, 2:3, :]
    y = DEEPNORM_ALPHA * x_ref[0] + (1.0 + gate) * mixed
    o_ref[0] = _layernorm(y, lng_ref[...], lnb_ref[...])


def _outproj(attn, cn, x, mod1, gattn, wo_a, wo_c, lng, lnb):
    b, s, d = x.shape
    tm = TM_PROJ
    const = lambda shape: pl.BlockSpec(shape, lambda bi, i: (0,) * len(shape))
    return pl.pallas_call(
        _outproj_kernel,
        out_shape=jax.ShapeDtypeStruct((b, s, d), F32),
        grid=(b, s // tm),
        in_specs=[pl.BlockSpec((1, tm, ATTN_WIDTH), lambda bi, i: (bi, i, 0)),
                  pl.BlockSpec((1, tm, CONV_WIDTH), lambda bi, i: (bi, i, 0)),
                  pl.BlockSpec((1, tm, d), lambda bi, i: (bi, i, 0)),
                  pl.BlockSpec((1, 3, d), lambda bi, i: (bi, 0, 0)),
                  const(gattn.shape), const(wo_a.shape), const(wo_c.shape),
                  const(lng.shape), const(lnb.shape)],
        out_specs=pl.BlockSpec((1, tm, d), lambda bi, i: (bi, i, 0)),
        compiler_params=pltpu.CompilerParams(
            dimension_semantics=("arbitrary", "arbitrary"), vmem_limit_bytes=VMEM_LIMIT),
        name="outproj",
    )(attn, cn, x, mod1, gattn, wo_a, wo_c, lng, lnb)


def _ffn_kernel(x_ref, mod_ref, w1_ref, w2_ref, lng_ref, lnb_ref, o_ref):
    x = x_ref[0]
    shift = mod_ref[0, 0:1, :]
    scale = mod_ref[0, 1:2, :]
    gate = mod_ref[0, 2:3, :]
    u = (x * (1.0 + scale) + shift).astype(BF16)
    ff = jnp.zeros(x.shape, F32)
    chunk = D_MODEL
    for c in range(D_FF // chunk):
        hid = jnp.maximum(_dot(u, w1_ref[:, c * chunk:(c + 1) * chunk]), 0.0)
        ff = ff + _dot((hid * hid).astype(BF16), w2_ref[c * chunk:(c + 1) * chunk, :])
    y = DEEPNORM_ALPHA * x + (1.0 + gate) * ff
    o_ref[0] = _layernorm(y, lng_ref[...], lnb_ref[...])


def _ffn(x1, mod2, w1, w2, lng, lnb):
    b, s, d = x1.shape
    tm = TM_PROJ
    const = lambda shape: pl.BlockSpec(shape, lambda bi, i: (0,) * len(shape),
                                       pipeline_mode=pl.Buffered(1))
    return pl.pallas_call(
        _ffn_kernel,
        out_shape=jax.ShapeDtypeStruct((b, s, d), F32),
        grid=(b, s // tm),
        in_specs=[pl.BlockSpec((1, tm, d), lambda bi, i: (bi, i, 0)),
                  pl.BlockSpec((1, 3, d), lambda bi, i: (bi, 0, 0)),
                  const(w1.shape), const(w2.shape), const(lng.shape), const(lnb.shape)],
        out_specs=pl.BlockSpec((1, tm, d), lambda bi, i: (bi, i, 0)),
        compiler_params=pltpu.CompilerParams(
            dimension_semantics=("arbitrary", "arbitrary"), vmem_limit_bytes=VMEM_LIMIT),
        name="ffn",
    )(x1, mod2, w1, w2, lng, lnb)


def kernel(x, c, w_ada, b_ada, w_in, b_forget, w_dw, b_dw, gn_g, gn_b, g_attn_out, g_conv_out,
           w_out, ln1_g, ln1_b, w_ff1, w_ff2, ln2_g, ln2_b):
    bsz = x.shape[0]
    layer = 0
    row = lambda v: v.reshape(1, -1)

    c_pad = jnp.pad(c, ((0, 8 - bsz), (0, 0)))
    ada = _ada(c_pad, w_ada[layer], row(b_ada[layer]))[:bsz]
    mods = ada.reshape(bsz, 6, D_MODEL)
    mod1, mod2 = mods[:, 0:3], mods[:, 3:6]

    w = w_in[layer]
    a0 = 3 * ATTN_WIDTH + N_HEADS
    wq, wk, wv = (w[:, i * ATTN_WIDTH:(i + 1) * ATTN_WIDTH] for i in range(3))
    wqv_t = jnp.concatenate([wq, wv], axis=1).T.astype(BF16)
    wk = wk.astype(BF16)
    reps = LANES // N_HEADS
    wf = jnp.tile(w[:, 3 * ATTN_WIDTH:a0], (1, reps)).astype(BF16)
    bf = jnp.tile(b_forget[layer], reps).reshape(1, LANES)
    wa = w[:, a0:a0 + CONV_WIDTH].astype(BF16)
    wg = w[:, a0 + CONV_WIDTH:].astype(BF16)

    pselq_t, pselk = _decay_routing_matrices()
    grp = np.arange(CONV_WIDTH) // CONV_GROUP
    gmat = jnp.asarray((grp[:, None] == grp[None, :]).astype(np.float32) / CONV_GROUP, BF16)
    qt, vt, k, augq_t, augk, cn = _inproj(
        x, mod1, wqv_t, wk, wf, bf, wa, wg, pselq_t, pselk,
        w_dw[layer].reshape(CONV_KERNEL, CONV_WIDTH), row(b_dw[layer]),
        row(gn_g[layer]), row(gn_b[layer]), row(g_conv_out[layer]), gmat)

    attn = _attn(qt, augq_t, k, augk, vt)

    wo = w_out[layer].astype(BF16)
    x1 = _outproj(attn, cn, x, mod1, row(g_attn_out[layer]), wo[:ATTN_WIDTH], wo[ATTN_WIDTH:],
                  row(ln1_g[layer]), row(ln1_b[layer]))

    return _ffn(x1, mod2, w_ff1[layer].astype(BF16), w_ff2[layer].astype(BF16),
                row(ln2_g[layer]), row(ln2_b[layer]))
```

```python
import functools

import numpy as np
import jax
import jax.numpy as jnp
from jax import lax
from jax.experimental import pallas as pl
from jax.experimental.pallas import tpu as pltpu

D_MODEL = 1024
HEAD_DIM = 64
ATTN_WIDTH = 512
CONV_WIDTH = 512
N_HEADS = 8
N_PAIRS = N_HEADS // 2
CONV_KERNEL = 31
CONV_GROUP = 64
D_FF = 4 * D_MODEL
LN_EPS = 1e-5
DEEPNORM_ALPHA = 2.0 ** 0.25

LANES = 128
SUBLANES = 8
HALO = 32
EXT_TAIL = 8
AUG_PER_HEAD = 6
AUG_ROWS = 16
NT_DIMS = (((1,), (1,)), ((), ()))
PIECE_LANE = (0, 8, 16)
ONE_LANE = 24
NEG_BIG = -1e30

TM_PROJ = 512
ROW_GROUP = 256
FF_CHUNK = 1024
TQ = 256
TK = 256
HEADS_PER_STEP = 8
KV_UNROLLS = (4, 2, 1)
LOG2E = 1.4426950408889634
VMEM_LIMIT = 56 * 1024 * 1024

F32 = jnp.float32
BF16 = jnp.bfloat16


def _split3(x):
    hi = x.astype(BF16)
    r = x - hi.astype(F32)
    mid = r.astype(BF16)
    lo = (r - mid.astype(F32)).astype(BF16)
    return hi, mid, lo


def _split2(x):
    hi = x.astype(BF16)
    lo = (x - hi.astype(F32)).astype(BF16)
    return hi, lo


def _dot(a, b):
    return jnp.dot(a, b, preferred_element_type=F32)


def _ada_kernel(c_ref, w_ref, b_ref, o_ref):
    c = c_ref[...]
    s = c * jax.nn.sigmoid(c)
    o_ref[...] = jnp.dot(s, w_ref[...], preferred_element_type=F32,
                         precision=lax.Precision.HIGHEST) + b_ref[...]


def _ada(c_pad, w_ada, b_ada):
    n = w_ada.shape[1]
    tn = 1536
    return pl.pallas_call(
        _ada_kernel,
        out_shape=jax.ShapeDtypeStruct((c_pad.shape[0], n), F32),
        grid=(n // tn,),
        in_specs=[pl.BlockSpec(c_pad.shape, lambda j: (0, 0)),
                  pl.BlockSpec((D_MODEL, tn), lambda j: (0, j)),
                  pl.BlockSpec((1, tn), lambda j: (0, j))],
        out_specs=pl.BlockSpec((c_pad.shape[0], tn), lambda j: (0, j)),
        name="ada",
    )(c_pad, w_ada, b_ada)


def _inproj_kernel(x_ref, mod_ref, wqv_t_ref, wk_ref, wf_ref, bf_ref, wa_ref, wg_ref,
                   pselq_t_ref, pselk_ref, wdw_ref, bdw_ref, gng_ref, gnb_ref, gout_ref, gmat_ref,
                   qt_ref, vt_ref, k_ref, augq_t_ref, augk_ref, cn_ref, carry_ref, ext_ref):
    tm = x_ref.shape[1]

    @pl.when(pl.program_id(1) == 0)
    def _():
        carry_ref[...] = jnp.zeros_like(carry_ref)
        ext_ref[0:HALO, :] = jnp.zeros((HALO, CONV_WIDTH), F32)
        ext_ref[HALO + tm:, :] = jnp.zeros((EXT_TAIL, CONV_WIDTH), F32)

    x = x_ref[0]
    shift = mod_ref[0, 0:1, :]
    scale = mod_ref[0, 1:2, :]
    u = (x * (1.0 + scale) + shift).astype(BF16)

    ext_ref[HALO:HALO + tm, :] = _dot(u, wa_ref[...]) * jax.nn.sigmoid(_dot(u, wg_ref[...]))

    def qv_rows(lo, hi):
        return lax.dot_general(wqv_t_ref[lo:hi, :], u, NT_DIMS, preferred_element_type=F32)

    def emit_q(lo, hi):
        qt_ref[0, lo:hi, :] = (qv_rows(lo, hi) * (LOG2E * HEAD_DIM ** -0.5)).astype(BF16)

    def emit_v(lo, hi):
        v_t = qv_rows(ATTN_WIDTH + lo, ATTN_WIDTH + hi).astype(BF16)
        for t in range(tm // TK):
            vt_ref[0, t, lo:hi, :] = v_t[:, t * TK:(t + 1) * TK]

    def emit_k():
        k_ref[0] = _dot(u, wk_ref[...]).astype(BF16)

    def emit_decay():
        fl = _dot(u, wf_ref[...]) + bf_ref[...]
        log_f = jnp.minimum(fl, 0.0) - jnp.log(1.0 + jnp.exp(-jnp.abs(fl)))
        row = lax.broadcasted_iota(jnp.int32, (tm, tm), 0)
        col = lax.broadcasted_iota(jnp.int32, (tm, tm), 1)
        tri = jnp.where(row >= col, 1.0, 0.0).astype(BF16)
        h, m, l = _split3(log_f)
        cum = carry_ref[...] + ((_dot(tri, h) + _dot(tri, m)) + _dot(tri, l))
        carry_ref[...] = cum[tm - 1:tm, :]
        lane = lax.broadcasted_iota(jnp.int32, (tm, LANES), 1)
        ch, cm, cl = (t.astype(F32) for t in _split3(cum * LOG2E))
        pieces = jnp.where(lane < PIECE_LANE[1], ch,
                           jnp.where(lane < PIECE_LANE[2], cm,
                                     jnp.where(lane < ONE_LANE, cl,
                                               jnp.where(lane == ONE_LANE, 1.0, 0.0))))
        pieces = pieces.astype(BF16)
        augk_ref[0] = _dot(pieces, pselk_ref[...]).astype(BF16)
        augq_t_ref[0] = lax.dot_general(pselq_t_ref[...], pieces, NT_DIMS,
                                        preferred_element_type=F32).astype(BF16)

    half = ATTN_WIDTH // 2
    matmul_items = [lambda: emit_q(0, half), lambda: emit_q(half, ATTN_WIDTH),
                    lambda: emit_v(0, half), lambda: emit_v(half, ATTN_WIDTH),
                    emit_k, emit_decay]

    first = HALO - (CONV_KERNEL - 1)
    rows = tm + 2 * SUBLANES
    y = jnp.zeros((tm, CONV_WIDTH), F32) + bdw_ref[...]
    for b in range(SUBLANES):
        zb = None
        for a in range(pl.cdiv(CONV_KERNEL - b, SUBLANES)):
            k = SUBLANES * a + b
            term = wdw_ref[k:k + 1, :] * ext_ref[SUBLANES * a:SUBLANES * a + rows, :]
            zb = term if zb is None else zb + term
        y = y + zb[first + b:first + b + tm, :]
        if b < len(matmul_items):
            matmul_items[b]()
    ext_ref[0:HALO, :] = ext_ref[tm:tm + HALO, :]
    cn_ref[0] = _norm_conv_branch(y, gng_ref, gnb_ref, gout_ref, gmat_ref)


def _norm_conv_branch(y, gng_ref, gnb_ref, gout_ref, gmat_ref):
    gmat = gmat_ref[...]
    yh, yl = _split2(y)
    mu = _dot(yh, gmat) + _dot(yl, gmat)
    d = y - mu
    dh, dl = _split2(d * d)
    var = _dot(dh, gmat) + _dot(dl, gmat)
    yn = d * lax.rsqrt(var + LN_EPS) * gng_ref[...] + gnb_ref[...]
    sw = yn * jax.nn.sigmoid(yn)
    ms = jnp.mean(sw * sw, axis=-1, keepdims=True)
    return (sw * lax.rsqrt(ms + LN_EPS) * gout_ref[...]).astype(BF16)


def _decay_routing_matrices():
    selq_t = np.zeros((N_HEADS * AUG_ROWS, LANES), np.float32)
    selk = np.zeros((LANES, N_PAIRS * LANES), np.float32)
    for head in range(N_HEADS):
        p, j = divmod(head, 2)
        qbase = head * AUG_ROWS + AUG_PER_HEAD * j
        kbase = p * LANES + AUG_PER_HEAD * j
        for i in range(3):
            selq_t[qbase + i, PIECE_LANE[i] + head] = 1.0
            selq_t[qbase + 3 + i, ONE_LANE] = 1.0
            selk[ONE_LANE, kbase + i] = 1.0
            selk[PIECE_LANE[i] + head, kbase + 3 + i] = -1.0
    return jnp.asarray(selq_t, BF16), jnp.asarray(selk, BF16)


def _inproj(x, mod1, wqv_t, wk, wf, bf, wa, wg, pselq_t, pselk, *conv_params):
    b, s, d = x.shape
    tm = TM_PROJ
    const = lambda shape: pl.BlockSpec(shape, lambda bi, i: (0,) * len(shape))
    conv_specs = [const(p.shape) for p in conv_params]
    return pl.pallas_call(
        _inproj_kernel,
        out_shape=(jax.ShapeDtypeStruct((b, ATTN_WIDTH, s), BF16),
                   jax.ShapeDtypeStruct((b, s // TK, ATTN_WIDTH, TK), BF16),
                   jax.ShapeDtypeStruct((b, s, ATTN_WIDTH), BF16),
                   jax.ShapeDtypeStruct((b, N_HEADS * AUG_ROWS, s), BF16),
                   jax.ShapeDtypeStruct((b, s, N_PAIRS * LANES), BF16),
                   jax.ShapeDtypeStruct((b, s, CONV_WIDTH), BF16)),
        grid=(b, s // tm),
        in_specs=[pl.BlockSpec((1, tm, d), lambda bi, i: (bi, i, 0)),
                  pl.BlockSpec((1, 3, d), lambda bi, i: (bi, 0, 0)),
                  const(wqv_t.shape), const(wk.shape), const(wf.shape), const(bf.shape),
                  const(wa.shape), const(wg.shape), const(pselq_t.shape), const(pselk.shape)]
                 + conv_specs,
        out_specs=(pl.BlockSpec((1, ATTN_WIDTH, tm), lambda bi, i: (bi, 0, i)),
                   pl.BlockSpec((1, tm // TK, ATTN_WIDTH, TK), lambda bi, i: (bi, i, 0, 0)),
                   pl.BlockSpec((1, tm, ATTN_WIDTH), lambda bi, i: (bi, i, 0)),
                   pl.BlockSpec((1, N_HEADS * AUG_ROWS, tm), lambda bi, i: (bi, 0, i)),
                   pl.BlockSpec((1, tm, N_PAIRS * LANES), lambda bi, i: (bi, i, 0)),
                   pl.BlockSpec((1, tm, CONV_WIDTH), lambda bi, i: (bi, i, 0))),
        scratch_shapes=[pltpu.VMEM((1, LANES), F32),
                        pltpu.VMEM((HALO + tm + EXT_TAIL, CONV_WIDTH), F32)],
        compiler_params=pltpu.CompilerParams(
            dimension_semantics=("arbitrary", "arbitrary"), vmem_limit_bytes=VMEM_LIMIT),
        name="inproj",
    )(x, mod1, wqv_t, wk, wf, bf, wa, wg, pselq_t, pselk, *conv_params)


def _attn_kernel(qt_ref, aqt_ref, k_ref, ak_ref, vt_ref, o_ref, s_scr, p_scr, acc_scr, st_scr):
    qi = pl.program_id(2)
    tq = qt_ref.shape[2]
    assert tq == TK, "the drain handles exactly one diagonal key block"
    heads = range(HEADS_PER_STEP)

    zeros_half = jnp.zeros((HEAD_DIM, tq), BF16)
    zeros_tail = jnp.zeros((LANES - AUG_ROWS, tq), BF16)
    qcat_t = []
    for h in heads:
        qh = qt_ref[0, HEAD_DIM * h:HEAD_DIM * (h + 1), :]
        halves = [qh, zeros_half] if h % 2 == 0 else [zeros_half, qh]
        qcat_t.append(jnp.concatenate(
            halves + [aqt_ref[0, AUG_ROWS * h:AUG_ROWS * (h + 1), :], zeros_tail], axis=0))

    def logits(kb, h):
        start = pl.multiple_of(kb * TK, TK)
        pair = pl.ds((h // 2) * LANES, LANES)
        kcat = jnp.concatenate([k_ref[0, pl.ds(start, TK), pair], ak_ref[0, pl.ds(start, TK), pair]],
                               axis=-1)
        return _dot(kcat, qcat_t[h])

    def block_max(s):
        return jnp.max(s, axis=0, keepdims=True)

    def softmax_update(s, m_blk, m):
        m_new = jnp.maximum(m, m_blk)
        alpha = jnp.exp2(m - m_new)
        p = jnp.exp2((s - m_new).astype(BF16))
        return p, alpha, m_new

    ones_rows = jnp.ones((AUG_ROWS, TK), BF16)

    def accumulate(kb, h, p, alpha, acc):
        vt = jnp.concatenate([vt_ref[0, kb, HEAD_DIM * h:HEAD_DIM * (h + 1), :], ones_rows], axis=0)
        return alpha * acc + _dot(vt, p)

    def stat(h, j):
        return st_scr.at[pl.ds(3 * h + j, 1), :]

    for h in heads:
        s0 = logits(0, h)
        s_scr[h] = s0
        p_scr[h] = jnp.zeros((TK, tq), BF16)
        acc_scr[h] = jnp.zeros((HEAD_DIM + AUG_ROWS, tq), F32)
        stat(h, 0)[...] = jnp.full((1, tq), NEG_BIG, F32)
        stat(h, 1)[...] = block_max(s0)
        stat(h, 2)[...] = jnp.ones((1, tq), F32)

    def step(kb):
        for h in heads:
            s_next = logits(kb + 1, h)
            p, alpha, m = softmax_update(s_scr[h], stat(h, 1)[...], stat(h, 0)[...])
            acc_scr[h] = accumulate(jnp.maximum(kb - 1, 0), h, p_scr[h], stat(h, 2)[...], acc_scr[h])
            s_scr[h] = s_next
            p_scr[h] = p
            stat(h, 0)[...] = m
            stat(h, 1)[...] = block_max(s_next)
            stat(h, 2)[...] = alpha

    done = 0
    for unroll in KV_UNROLLS:
        trips = (qi - done) // unroll

        def body(i, carry, unroll=unroll, base=done):
            for u in range(unroll):
                step(base + unroll * i + u)
            return carry

        lax.fori_loop(0, trips, body, 0)
        done = done + trips * unroll

    key = lax.broadcasted_iota(jnp.int32, (TK, tq), 0)
    qry = lax.broadcasted_iota(jnp.int32, (TK, tq), 1)
    outs = []
    for h in heads:
        s = jnp.where(key <= qry, s_scr[h], NEG_BIG)
        p, alpha, _ = softmax_update(s, block_max(s), stat(h, 0)[...])
        acc = accumulate(jnp.maximum(qi - 1, 0), h, p_scr[h], stat(h, 2)[...], acc_scr[h])
        acc = accumulate(qi, h, p, alpha, acc)
        outs.append(acc[:HEAD_DIM] / acc[HEAD_DIM:HEAD_DIM + 1])
    for pr in range(HEADS_PER_STEP // 2):
        o_ref[0, :, pr * LANES:(pr + 1) * LANES] = jnp.concatenate(outs[2 * pr:2 * pr + 2], axis=0).T


def _attn(qt, augq_t, k, augk, vt):
    b, s, _ = k.shape
    g = HEADS_PER_STEP
    return pl.pallas_call(
        _attn_kernel,
        out_shape=jax.ShapeDtypeStruct((b, s, ATTN_WIDTH), F32),
        grid=(b, N_HEADS // g, s // TQ),
        in_specs=[pl.BlockSpec((1, g * HEAD_DIM, TQ), lambda bi, p, i: (bi, p, i)),
                  pl.BlockSpec((1, g * AUG_ROWS, TQ), lambda bi, p, i: (bi, p, i)),
                  pl.BlockSpec((1, s, g * HEAD_DIM), lambda bi, p, i: (bi, 0, p),
                               pipeline_mode=pl.Buffered(1)),
                  pl.BlockSpec((1, s, g * HEAD_DIM), lambda bi, p, i: (bi, 0, p),
                               pipeline_mode=pl.Buffered(1)),
                  pl.BlockSpec((1, s // TK, g * HEAD_DIM, TK), lambda bi, p, i: (bi, 0, p, 0),
                               pipeline_mode=pl.Buffered(1))],
        out_specs=pl.BlockSpec((1, TQ, g * HEAD_DIM), lambda bi, p, i: (bi, i, p)),
        scratch_shapes=[pltpu.VMEM((g, TK, TQ), F32), pltpu.VMEM((g, TK, TQ), BF16),
                        pltpu.VMEM((g, HEAD_DIM + AUG_ROWS, TQ), F32), pltpu.VMEM((3 * g, TQ), F32)],
        compiler_params=pltpu.CompilerParams(
            dimension_semantics=("arbitrary", "arbitrary", "arbitrary"),
            vmem_limit_bytes=VMEM_LIMIT),
        name="attn",
    )(qt, augq_t, k, augk, vt)


def _layernorm(y, g, b):
    mu = jnp.mean(y, axis=-1, keepdims=True)
    d = y - mu
    var = jnp.mean(d * d, axis=-1, keepdims=True)
    return d * lax.rsqrt(var + LN_EPS) * g + b


def _mix_mlp_kernel(attn_ref, cn_ref, x_ref, mod_ref, gattn_ref, wo_a_ref, wo_c_ref,
                    ln1g_ref, ln1b_ref, w1_ref, w2_ref, ln2g_ref, ln2b_ref, o_ref):
    tm = x_ref.shape[1]
    groups = [slice(r, r + ROW_GROUP) for r in range(0, tm, ROW_GROUP)]
    gate1, shift2, scale2, gate2 = (mod_ref[0, j:j + 1, :] for j in range(2, 6))

    def token_mix(rows):
        a = attn_ref[0, rows, :]
        ms = jnp.mean(a * a, axis=-1, keepdims=True)
        an = (a * lax.rsqrt(ms + LN_EPS) * gattn_ref[...]).astype(BF16)
        mixed = _dot(an, wo_a_ref[...]) + _dot(cn_ref[0, rows, :], wo_c_ref[...])
        return DEEPNORM_ALPHA * x_ref[0, rows, :] + (1.0 + gate1) * mixed

    def mlp(x1):
        u = (x1 * (1.0 + scale2) + shift2).astype(BF16)
        ff = jnp.zeros(x1.shape, F32)
        for c in range(0, D_FF, FF_CHUNK):
            hid = jnp.maximum(_dot(u, w1_ref[:, c:c + FF_CHUNK]), 0.0)
            ff = ff + _dot((hid * hid).astype(BF16), w2_ref[c:c + FF_CHUNK, :])
        return DEEPNORM_ALPHA * x1 + (1.0 + gate2) * ff

    x1 = [_layernorm(token_mix(rows), ln1g_ref[...], ln1b_ref[...]) for rows in groups]
    for rows, x1_rows in zip(groups, x1):
        o_ref[0, rows, :] = _layernorm(mlp(x1_rows), ln2g_ref[...], ln2b_ref[...])


def _mix_mlp(attn, cn, x, mods, gattn, wo_a, wo_c, ln1g, ln1b, w1, w2, ln2g, ln2b):
    b, s, d = x.shape
    tm = TM_PROJ
    const = lambda a: pl.BlockSpec(a.shape, lambda bi, i: (0,) * a.ndim, pipeline_mode=pl.Buffered(1))
    return pl.pallas_call(
        _mix_mlp_kernel,
        out_shape=jax.ShapeDtypeStruct((b, s, d), F32),
        grid=(b, s // tm),
        in_specs=[pl.BlockSpec((1, tm, ATTN_WIDTH), lambda bi, i: (bi, i, 0)),
                  pl.BlockSpec((1, tm, CONV_WIDTH), lambda bi, i: (bi, i, 0)),
                  pl.BlockSpec((1, tm, d), lambda bi, i: (bi, i, 0)),
                  pl.BlockSpec((1, 6, d), lambda bi, i: (bi, 0, 0)),
                  const(gattn), const(wo_a), const(wo_c), const(ln1g), const(ln1b),
                  const(w1), const(w2), const(ln2g), const(ln2b)],
        out_specs=pl.BlockSpec((1, tm, d), lambda bi, i: (bi, i, 0)),
        compiler_params=pltpu.CompilerParams(
            dimension_semantics=("arbitrary", "arbitrary"), vmem_limit_bytes=VMEM_LIMIT),
        name="mix_mlp",
    )(attn, cn, x, mods, gattn, wo_a, wo_c, ln1g, ln1b, w1, w2, ln2g, ln2b)


def kernel(x, c, w_ada, b_ada, w_in, b_forget, w_dw, b_dw, gn_g, gn_b, g_attn_out, g_conv_out,
           w_out, ln1_g, ln1_b, w_ff1, w_ff2, ln2_g, ln2_b):
    bsz = x.shape[0]
    layer = 0
    row = lambda v: v.reshape(1, -1)

    c_pad = jnp.pad(c, ((0, 8 - bsz), (0, 0)))
    ada = _ada(c_pad, w_ada[layer], row(b_ada[layer]))[:bsz]
    mods = ada.reshape(bsz, 6, D_MODEL)
    mod1 = mods[:, 0:3]

    w = w_in[layer]
    a0 = 3 * ATTN_WIDTH + N_HEADS
    wq, wk, wv = (w[:, i * ATTN_WIDTH:(i + 1) * ATTN_WIDTH] for i in range(3))
    wqv_t = jnp.concatenate([wq, wv], axis=1).T.astype(BF16)
    wk = wk.astype(BF16)
    reps = LANES // N_HEADS
    wf = jnp.tile(w[:, 3 * ATTN_WIDTH:a0], (1, reps)).astype(BF16)
    bf = jnp.tile(b_forget[layer], reps).reshape(1, LANES)
    wa = w[:, a0:a0 + CONV_WIDTH].astype(BF16)
    wg = w[:, a0 + CONV_WIDTH:].astype(BF16)

    pselq_t, pselk = _decay_routing_matrices()
    grp = np.arange(CONV_WIDTH) // CONV_GROUP
    gmat = jnp.asarray((grp[:, None] == grp[None, :]).astype(np.float32) / CONV_GROUP, BF16)
    qt, vt, k, augq_t, augk, cn = _inproj(
        x, mod1, wqv_t, wk, wf, bf, wa, wg, pselq_t, pselk,
        w_dw[layer].reshape(CONV_KERNEL, CONV_WIDTH), row(b_dw[layer]),
        row(gn_g[layer]), row(gn_b[layer]), row(g_conv_out[layer]), gmat)

    attn = _attn(qt, augq_t, k, augk, vt)

    wo = w_out[layer].astype(BF16)
    return _mix_mlp(attn, cn, x, mods, row(g_attn_out[layer]), wo[:ATTN_WIDTH], wo[ATTN_WIDTH:],
                    row(ln1_g[layer]), row(ln1_b[layer]),
                    w_ff1[layer].astype(BF16), w_ff2[layer].astype(BF16),
                    row(ln2_g[layer]), row(ln2_b[layer]))
```

```python
import functools

import numpy as np
import jax
import jax.numpy as jnp
from jax import lax
from jax.experimental import pallas as pl
from jax.experimental.pallas import tpu as pltpu

D_MODEL = 1024
HEAD_DIM = 64
ATTN_WIDTH = 512
CONV_WIDTH = 512
N_HEADS = 8
N_PAIRS = N_HEADS // 2
CONV_KERNEL = 31
CONV_GROUP = 64
D_FF = 4 * D_MODEL
LN_EPS = 1e-5
DEEPNORM_ALPHA = 2.0 ** 0.25

LANES = 128
SUBLANES = 8
HALO = 32
EXT_TAIL = 8
AUG_PER_HEAD = 6
AUG_ROWS = 16
NT_DIMS = (((1,), (1,)), ((), ()))
PIECE_LANE = (0, 8, 16)
ONE_LANE = 24
NEG_BIG = -1e30

TM_PROJ = 512
ROW_GROUP = 256
FF_CHUNK = 1024
TQ = 256
TK = 256
HEADS_PER_STEP = 8
KV_UNROLLS = (4, 2, 1)
TILES_PER_STEP = 2
LOG2E = 1.4426950408889634
VMEM_LIMIT = 56 * 1024 * 1024

F32 = jnp.float32
BF16 = jnp.bfloat16


def _split3(x):
    hi = x.astype(BF16)
    r = x - hi.astype(F32)
    mid = r.astype(BF16)
    lo = (r - mid.astype(F32)).astype(BF16)
    return hi, mid, lo


def _split2(x):
    hi = x.astype(BF16)
    lo = (x - hi.astype(F32)).astype(BF16)
    return hi, lo


def _dot(a, b):
    return jnp.dot(a, b, preferred_element_type=F32)


def _ada_kernel(c_ref, w_ref, b_ref, o_ref):
    c = c_ref[...]
    sh, sl = _split2(c * jax.nn.sigmoid(c))
    wh, wl = _split2(w_ref[...])
    o_ref[...] = (_dot(sh, wh) + (_dot(sl, wh) + _dot(sh, wl))) + b_ref[...]


def _ada(c_pad, w_ada, b_ada):
    n = w_ada.shape[1]
    tn = 1536
    return pl.pallas_call(
        _ada_kernel,
        out_shape=jax.ShapeDtypeStruct((c_pad.shape[0], n), F32),
        grid=(n // tn,),
        in_specs=[pl.BlockSpec(c_pad.shape, lambda j: (0, 0)),
                  pl.BlockSpec((D_MODEL, tn), lambda j: (0, j)),
                  pl.BlockSpec((1, tn), lambda j: (0, j))],
        out_specs=pl.BlockSpec((c_pad.shape[0], tn), lambda j: (0, j)),
        name="ada",
    )(c_pad, w_ada, b_ada)


def _inproj_kernel(x_ref, mod_ref, wqv_t_ref, wk_ref, wf_ref, bf_ref, wa_ref, wg_ref,
                   pselq_t_ref, pselk_ref, wdw_ref, bdw_ref, gng_ref, gnb_ref, gout_ref, gmat_ref,
                   qt_ref, vt_ref, k_ref, augq_t_ref, augk_ref, cn_ref, carry_ref, ext_ref):
    tm = x_ref.shape[1]

    @pl.when(pl.program_id(1) == 0)
    def _():
        carry_ref[...] = jnp.zeros_like(carry_ref)
        ext_ref[0:HALO, :] = jnp.zeros((HALO, CONV_WIDTH), F32)
        ext_ref[HALO + tm:, :] = jnp.zeros((EXT_TAIL, CONV_WIDTH), F32)

    x = x_ref[0]
    shift = mod_ref[0, 0:1, :]
    scale = mod_ref[0, 1:2, :]
    u = (x * (1.0 + scale) + shift).astype(BF16)

    ext_ref[HALO:HALO + tm, :] = _dot(u, wa_ref[...]) * jax.nn.sigmoid(_dot(u, wg_ref[...]))

    def qv_rows(lo, hi):
        return lax.dot_general(wqv_t_ref[lo:hi, :], u, NT_DIMS, preferred_element_type=F32)

    def emit_q(lo, hi):
        qt_ref[0, lo:hi, :] = (qv_rows(lo, hi) * (LOG2E * HEAD_DIM ** -0.5)).astype(BF16)

    def emit_v(lo, hi):
        v_t = qv_rows(ATTN_WIDTH + lo, ATTN_WIDTH + hi).astype(BF16)
        for t in range(tm // TK):
            vt_ref[0, t, lo:hi, :] = v_t[:, t * TK:(t + 1) * TK]

    def emit_k():
        k_ref[0] = _dot(u, wk_ref[...]).astype(BF16)

    def emit_decay():
        fl = _dot(u, wf_ref[...]) + bf_ref[...]
        log_f = jnp.minimum(fl, 0.0) - jnp.log(1.0 + jnp.exp(-jnp.abs(fl)))
        row = lax.broadcasted_iota(jnp.int32, (tm, tm), 0)
        col = lax.broadcasted_iota(jnp.int32, (tm, tm), 1)
        tri = jnp.where(row >= col, 1.0, 0.0).astype(BF16)
        h, m, l = _split3(log_f)
        cum = carry_ref[...] + ((_dot(tri, h) + _dot(tri, m)) + _dot(tri, l))
        carry_ref[...] = cum[tm - 1:tm, :]
        lane = lax.broadcasted_iota(jnp.int32, (tm, LANES), 1)
        ch, cm, cl = (t.astype(F32) for t in _split3(cum * LOG2E))
        pieces = jnp.where(lane < PIECE_LANE[1], ch,
                           jnp.where(lane < PIECE_LANE[2], cm,
                                     jnp.where(lane < ONE_LANE, cl,
                                               jnp.where(lane == ONE_LANE, 1.0, 0.0))))
        pieces = pieces.astype(BF16)
        augk_ref[0] = _dot(pieces, pselk_ref[...]).astype(BF16)
        augq_t_ref[0] = lax.dot_general(pselq_t_ref[...], pieces, NT_DIMS,
                                        preferred_element_type=F32).astype(BF16)

    half = ATTN_WIDTH // 2
    matmul_items = [lambda: emit_q(0, half), lambda: emit_q(half, ATTN_WIDTH),
                    lambda: emit_v(0, half), lambda: emit_v(half, ATTN_WIDTH),
                    emit_k, emit_decay]

    first = HALO - (CONV_KERNEL - 1)
    rows = tm + 2 * SUBLANES
    y = jnp.zeros((tm, CONV_WIDTH), F32) + bdw_ref[...]
    for b in range(SUBLANES):
        zb = None
        for a in range(pl.cdiv(CONV_KERNEL - b, SUBLANES)):
            k = SUBLANES * a + b
            term = wdw_ref[k:k + 1, :] * ext_ref[SUBLANES * a:SUBLANES * a + rows, :]
            zb = term if zb is None else zb + term
        y = y + zb[first + b:first + b + tm, :]
        if b < len(matmul_items):
            matmul_items[b]()
    ext_ref[0:HALO, :] = ext_ref[tm:tm + HALO, :]
    for r in range(0, tm, ROW_GROUP):
        cn_ref[0, r:r + ROW_GROUP, :] = _norm_conv_branch(y[r:r + ROW_GROUP], gng_ref, gnb_ref,
                                                          gout_ref, gmat_ref)


def _norm_conv_branch(y, gng_ref, gnb_ref, gout_ref, gmat_ref):
    gmat = gmat_ref[...]
    yh, yl = _split2(y)
    mu = _dot(yh, gmat) + _dot(yl, gmat)
    d = y - mu
    dh, dl = _split2(d * d)
    var = _dot(dh, gmat) + _dot(dl, gmat)
    yn = d * lax.rsqrt(var + LN_EPS) * gng_ref[...] + gnb_ref[...]
    sw = yn * jax.nn.sigmoid(yn)
    ms = jnp.mean(sw * sw, axis=-1, keepdims=True)
    return (sw * lax.rsqrt(ms + LN_EPS) * gout_ref[...]).astype(BF16)


def _decay_routing_matrices():
    selq_t = np.zeros((N_HEADS * AUG_ROWS, LANES), np.float32)
    selk = np.zeros((LANES, N_PAIRS * LANES), np.float32)
    for head in range(N_HEADS):
        p, j = divmod(head, 2)
        qbase = head * AUG_ROWS + AUG_PER_HEAD * j
        kbase = p * LANES + AUG_PER_HEAD * j
        for i in range(3):
            selq_t[qbase + i, PIECE_LANE[i] + head] = 1.0
            selq_t[qbase + 3 + i, ONE_LANE] = 1.0
            selk[ONE_LANE, kbase + i] = 1.0
            selk[PIECE_LANE[i] + head, kbase + 3 + i] = -1.0
    return jnp.asarray(selq_t, BF16), jnp.asarray(selk, BF16)


def _inproj(x, mod1, wqv_t, wk, wf, bf, wa, wg, pselq_t, pselk, *conv_params):
    b, s, d = x.shape
    tm = TM_PROJ
    const = lambda shape: pl.BlockSpec(shape, lambda bi, i: (0,) * len(shape))
    conv_specs = [const(p.shape) for p in conv_params]
    return pl.pallas_call(
        _inproj_kernel,
        out_shape=(jax.ShapeDtypeStruct((b, ATTN_WIDTH, s), BF16),
                   jax.ShapeDtypeStruct((b, s // TK, ATTN_WIDTH, TK), BF16),
                   jax.ShapeDtypeStruct((b, s, ATTN_WIDTH), BF16),
                   jax.ShapeDtypeStruct((b, N_HEADS * AUG_ROWS, s), BF16),
                   jax.ShapeDtypeStruct((b, s, N_PAIRS * LANES), BF16),
                   jax.ShapeDtypeStruct((b, s, CONV_WIDTH), BF16)),
        grid=(b, s // tm),
        in_specs=[pl.BlockSpec((1, tm, d), lambda bi, i: (bi, i, 0)),
                  pl.BlockSpec((1, 3, d), lambda bi, i: (bi, 0, 0)),
                  const(wqv_t.shape), const(wk.shape), const(wf.shape), const(bf.shape),
                  const(wa.shape), const(wg.shape), const(pselq_t.shape), const(pselk.shape)]
                 + conv_specs,
        out_specs=(pl.BlockSpec((1, ATTN_WIDTH, tm), lambda bi, i: (bi, 0, i)),
                   pl.BlockSpec((1, tm // TK, ATTN_WIDTH, TK), lambda bi, i: (bi, i, 0, 0)),
                   pl.BlockSpec((1, tm, ATTN_WIDTH), lambda bi, i: (bi, i, 0)),
                   pl.BlockSpec((1, N_HEADS * AUG_ROWS, tm), lambda bi, i: (bi, 0, i)),
                   pl.BlockSpec((1, tm, N_PAIRS * LANES), lambda bi, i: (bi, i, 0)),
                   pl.BlockSpec((1, tm, CONV_WIDTH), lambda bi, i: (bi, i, 0))),
        scratch_shapes=[pltpu.VMEM((1, LANES), F32),
                        pltpu.VMEM((HALO + tm + EXT_TAIL, CONV_WIDTH), F32)],
        compiler_params=pltpu.CompilerParams(
            dimension_semantics=("arbitrary", "arbitrary"), vmem_limit_bytes=VMEM_LIMIT),
        name="inproj",
    )(x, mod1, wqv_t, wk, wf, bf, wa, wg, pselq_t, pselk, *conv_params)


def _attn_kernel(qt_ref, aqt_ref, k_ref, ak_ref, vt_ref, o_ref, s_scr, p_scr, acc_scr, st_scr):
    for sub in range(TILES_PER_STEP):
        cols = slice(sub * TQ, (sub + 1) * TQ)
        _attn_tile(pl.program_id(2) * TILES_PER_STEP + sub,
                   qt_ref.at[0, :, cols], aqt_ref.at[0, :, cols], k_ref, ak_ref, vt_ref,
                   o_ref.at[0, cols, :], s_scr.at[sub], p_scr.at[sub], acc_scr.at[sub], st_scr.at[sub])


def _attn_tile(qi, qt_ref, aqt_ref, k_ref, ak_ref, vt_ref, o_ref, s_scr, p_scr, acc_scr, st_scr):
    tq = qt_ref.shape[1]
    assert tq == TK, "the drain handles exactly one diagonal key block"
    heads = range(HEADS_PER_STEP)

    zeros_half = jnp.zeros((HEAD_DIM, tq), BF16)
    zeros_tail = jnp.zeros((LANES - AUG_ROWS, tq), BF16)
    qcat_t = []
    for h in heads:
        qh = qt_ref[HEAD_DIM * h:HEAD_DIM * (h + 1), :]
        halves = [qh, zeros_half] if h % 2 == 0 else [zeros_half, qh]
        qcat_t.append(jnp.concatenate(
            halves + [aqt_ref[AUG_ROWS * h:AUG_ROWS * (h + 1), :], zeros_tail], axis=0))

    def logits(kb, h):
        start = pl.multiple_of(kb * TK, TK)
        pair = pl.ds((h // 2) * LANES, LANES)
        kcat = jnp.concatenate([k_ref[0, pl.ds(start, TK), pair], ak_ref[0, pl.ds(start, TK), pair]],
                               axis=-1)
        return _dot(kcat, qcat_t[h])

    def block_max(s):
        return jnp.max(s, axis=0, keepdims=True)

    def softmax_update(s, m_blk, m):
        m_new = jnp.maximum(m, m_blk)
        alpha = jnp.exp2(m - m_new)
        p = jnp.exp2((s - m_new).astype(BF16))
        return p, alpha, m_new

    ones_rows = jnp.ones((AUG_ROWS, TK), BF16)

    def accumulate(kb, h, p, alpha, acc):
        vt = jnp.concatenate([vt_ref[0, kb, HEAD_DIM * h:HEAD_DIM * (h + 1), :], ones_rows], axis=0)
        return alpha * acc + _dot(vt, p)

    def stat(h, j):
        return st_scr.at[pl.ds(3 * h + j, 1), :]

    for h in heads:
        s0 = logits(0, h)
        s_scr[h] = s0
        p_scr[h] = jnp.zeros((TK, tq), BF16)
        acc_scr[h] = jnp.zeros((HEAD_DIM + AUG_ROWS, tq), F32)
        stat(h, 0)[...] = jnp.full((1, tq), NEG_BIG, F32)
        stat(h, 1)[...] = block_max(s0)
        stat(h, 2)[...] = jnp.ones((1, tq), F32)

    def step(kb):
        for h in heads:
            s_next = logits(kb + 1, h)
            p, alpha, m = softmax_update(s_scr[h], stat(h, 1)[...], stat(h, 0)[...])
            acc_scr[h] = accumulate(jnp.maximum(kb - 1, 0), h, p_scr[h], stat(h, 2)[...], acc_scr[h])
            s_scr[h] = s_next
            p_scr[h] = p
            stat(h, 0)[...] = m
            stat(h, 1)[...] = block_max(s_next)
            stat(h, 2)[...] = alpha

    done = 0
    for unroll in KV_UNROLLS:
        trips = (qi - done) // unroll

        def body(i, carry, unroll=unroll, base=done):
            for u in range(unroll):
                step(base + unroll * i + u)
            return carry

        lax.fori_loop(0, trips, body, 0)
        done = done + trips * unroll

    key = lax.broadcasted_iota(jnp.int32, (TK, tq), 0)
    qry = lax.broadcasted_iota(jnp.int32, (TK, tq), 1)
    outs = []
    for h in heads:
        s = jnp.where(key <= qry, s_scr[h], NEG_BIG)
        p, alpha, _ = softmax_update(s, block_max(s), stat(h, 0)[...])
        acc = accumulate(jnp.maximum(qi - 1, 0), h, p_scr[h], stat(h, 2)[...], acc_scr[h])
        acc = accumulate(qi, h, p, alpha, acc)
        outs.append(acc[:HEAD_DIM] / acc[HEAD_DIM:HEAD_DIM + 1])
    for pr in range(HEADS_PER_STEP // 2):
        o_ref[:, pr * LANES:(pr + 1) * LANES] = jnp.concatenate(outs[2 * pr:2 * pr + 2], axis=0).T


def _attn(qt, augq_t, k, augk, vt):
    b, s, _ = k.shape
    g, n = HEADS_PER_STEP, TILES_PER_STEP
    tq_step = n * TQ
    return pl.pallas_call(
        _attn_kernel,
        out_shape=jax.ShapeDtypeStruct((b, s, ATTN_WIDTH), F32),
        grid=(b, N_HEADS // g, s // tq_step),
        in_specs=[pl.BlockSpec((1, g * HEAD_DIM, tq_step), lambda bi, p, i: (bi, p, i)),
                  pl.BlockSpec((1, g * AUG_ROWS, tq_step), lambda bi, p, i: (bi, p, i)),
                  pl.BlockSpec((1, s, g * HEAD_DIM), lambda bi, p, i: (bi, 0, p),
                               pipeline_mode=pl.Buffered(1)),
                  pl.BlockSpec((1, s, g * HEAD_DIM), lambda bi, p, i: (bi, 0, p),
                               pipeline_mode=pl.Buffered(1)),
                  pl.BlockSpec((1, s // TK, g * HEAD_DIM, TK), lambda bi, p, i: (bi, 0, p, 0),
                               pipeline_mode=pl.Buffered(1))],
        out_specs=pl.BlockSpec((1, tq_step, g * HEAD_DIM), lambda bi, p, i: (bi, i, p)),
        scratch_shapes=[pltpu.VMEM((n, g, TK, TQ), F32), pltpu.VMEM((n, g, TK, TQ), BF16),
                        pltpu.VMEM((n, g, HEAD_DIM + AUG_ROWS, TQ), F32),
                        pltpu.VMEM((n, 3 * g, TQ), F32)],
        compiler_params=pltpu.CompilerParams(
            dimension_semantics=("arbitrary", "arbitrary", "arbitrary"),
            vmem_limit_bytes=VMEM_LIMIT),
        name="attn",
    )(qt, augq_t, k, augk, vt)


def _layernorm(y, g, b):
    mu = jnp.mean(y, axis=-1, keepdims=True)
    d = y - mu
    var = jnp.mean(d * d, axis=-1, keepdims=True)
    return d * lax.rsqrt(var + LN_EPS) * g + b


def _mix_mlp_kernel(attn_ref, cn_ref, x_ref, mod_ref, gattn_ref, wo_a_ref, wo_c_ref,
                    ln1g_ref, ln1b_ref, w1_ref, w2_ref, ln2g_ref, ln2b_ref, o_ref):
    tm = x_ref.shape[1]
    groups = [slice(r, r + ROW_GROUP) for r in range(0, tm, ROW_GROUP)]
    gate1, shift2, scale2, gate2 = (mod_ref[0, j:j + 1, :] for j in range(2, 6))

    def token_mix(rows):
        a = attn_ref[0, rows, :]
        ms = jnp.mean(a * a, axis=-1, keepdims=True)
        an = (a * lax.rsqrt(ms + LN_EPS) * gattn_ref[...]).astype(BF16)
        mixed = _dot(an, wo_a_ref[...]) + _dot(cn_ref[0, rows, :], wo_c_ref[...])
        return DEEPNORM_ALPHA * x_ref[0, rows, :] + (1.0 + gate1) * mixed

    def mlp(x1):
        u = (x1 * (1.0 + scale2) + shift2).astype(BF16)
        ff = jnp.zeros(x1.shape, F32)
        for c in range(0, D_FF, FF_CHUNK):
            hid = jnp.maximum(_dot(u, w1_ref[:, c:c + FF_CHUNK]), 0.0)
            ff = ff + _dot((hid * hid).astype(BF16), w2_ref[c:c + FF_CHUNK, :])
        return DEEPNORM_ALPHA * x1 + (1.0 + gate2) * ff

    x1 = [_layernorm(token_mix(rows), ln1g_ref[...], ln1b_ref[...]) for rows in groups]
    for rows, x1_rows in zip(groups, x1):
        o_ref[0, rows, :] = _layernorm(mlp(x1_rows), ln2g_ref[...], ln2b_ref[...])


def _mix_mlp(attn, cn, x, mods, gattn, wo_a, wo_c, ln1g, ln1b, w1, w2, ln2g, ln2b):
    b, s, d = x.shape
    tm = TM_PROJ
    const = lambda a: pl.BlockSpec(a.shape, lambda bi, i: (0,) * a.ndim, pipeline_mode=pl.Buffered(1))
    return pl.pallas_call(
        _mix_mlp_kernel,
        out_shape=jax.ShapeDtypeStruct((b, s, d), F32),
        grid=(b, s // tm),
        in_specs=[pl.BlockSpec((1, tm, ATTN_WIDTH), lambda bi, i: (bi, i, 0)),
                  pl.BlockSpec((1, tm, CONV_WIDTH), lambda bi, i: (bi, i, 0)),
                  pl.BlockSpec((1, tm, d), lambda bi, i: (bi, i, 0)),
                  pl.BlockSpec((1, 6, d), lambda bi, i: (bi, 0, 0)),
                  const(gattn), const(wo_a), const(wo_c), const(ln1g), const(ln1b),
                  const(w1), const(w2), const(ln2g), const(ln2b)],
        out_specs=pl.BlockSpec((1, tm, d), lambda bi, i: (bi, i, 0)),
        compiler_params=pltpu.CompilerParams(
            dimension_semantics=("arbitrary", "arbitrary"), vmem_limit_bytes=VMEM_LIMIT),
        name="mix_mlp",
    )(attn, cn, x, mods, gattn, wo_a, wo_c, ln1g, ln1b, w1, w2, ln2g, ln2b)


def kernel(x, c, w_ada, b_ada, w_in, b_forget, w_dw, b_dw, gn_g, gn_b, g_attn_out, g_conv_out,
           w_out, ln1_g, ln1_b, w_ff1, w_ff2, ln2_g, ln2_b):
    bsz = x.shape[0]
    layer = 0
    row = lambda v: v.reshape(1, -1)

    c_pad = jnp.pad(c, ((0, 2 * SUBLANES - bsz), (0, 0)))
    ada = _ada(c_pad, w_ada[layer], row(b_ada[layer]))[:bsz]
    mods = ada.reshape(bsz, 6, D_MODEL)
    mod1 = mods[:, 0:3]

    w = w_in[layer]
    a0 = 3 * ATTN_WIDTH + N_HEADS
    wq, wk, wv = (w[:, i * ATTN_WIDTH:(i + 1) * ATTN_WIDTH] for i in range(3))
    wqv_t = jnp.concatenate([wq, wv], axis=1).T.astype(BF16)
    wk = wk.astype(BF16)
    reps = LANES // N_HEADS
    wf = jnp.tile(w[:, 3 * ATTN_WIDTH:a0], (1, reps)).astype(BF16)
    bf = jnp.tile(b_forget[layer], reps).reshape(1, LANES)
    wa = w[:, a0:a0 + CONV_WIDTH].astype(BF16)
    wg = w[:, a0 + CONV_WIDTH:].astype(BF16)

    pselq_t, pselk = _decay_routing_matrices()
    grp = np.arange(CONV_WIDTH) // CONV_GROUP
    gmat = jnp.asarray((grp[:, None] == grp[None, :]).astype(np.float32) / CONV_GROUP, BF16)
    qt, vt, k, augq_t, augk, cn = _inproj(
        x, mod1, wqv_t, wk, wf, bf, wa, wg, pselq_t, pselk,
        w_dw[layer].reshape(CONV_KERNEL, CONV_WIDTH), row(b_dw[layer]),
        row(gn_g[layer]), row(gn_b[layer]), row(g_conv_out[layer]), gmat)

    attn = _attn(qt, augq_t, k, augk, vt)

    wo = w_out[layer].astype(BF16)
    return _mix_mlp(attn, cn, x, mods, row(g_attn_out[layer]), wo[:ATTN_WIDTH], wo[ATTN_WIDTH:],
                    row(ln1_g[layer]), row(ln1_b[layer]),
                    w_ff1[layer].astype(BF16), w_ff2[layer].astype(BF16),
                    row(ln2_g[layer]), row(ln2_b[layer]))
```

```python
import functools

import numpy as np
import jax
import jax.numpy as jnp
from jax import lax
from jax.experimental import pallas as pl
from jax.experimental.pallas import tpu as pltpu

D_MODEL = 1024
HEAD_DIM = 64
ATTN_WIDTH = 512
CONV_WIDTH = 512
N_HEADS = 8
N_PAIRS = N_HEADS // 2
CONV_KERNEL = 31
CONV_GROUP = 64
D_FF = 4 * D_MODEL
LN_EPS = 1e-5
DEEPNORM_ALPHA = 2.0 ** 0.25

LANES = 128
SUBLANES = 8
HALO = 32
EXT_TAIL = 8
AUG_PER_HEAD = 6
AUG_ROWS = 16
NT_DIMS = (((1,), (1,)), ((), ()))
PIECE_LANE = (0, 8, 16)
ONE_LANE = 24
NEG_BIG = -1e30

TM_PROJ = 512
ROW_GROUP = 256
FF_CHUNK = 1024
TQ = 256
TK = 256
HEADS_PER_STEP = 8
KV_UNROLL = 4
TILES_PER_STEP = 4
LOG2E = 1.4426950408889634
VMEM_LIMIT = 56 * 1024 * 1024

F32 = jnp.float32
BF16 = jnp.bfloat16


def _split3(x):
    hi = x.astype(BF16)
    r = x - hi.astype(F32)
    mid = r.astype(BF16)
    lo = (r - mid.astype(F32)).astype(BF16)
    return hi, mid, lo


def _split2(x):
    hi = x.astype(BF16)
    lo = (x - hi.astype(F32)).astype(BF16)
    return hi, lo


def _dot(a, b):
    return jnp.dot(a, b, preferred_element_type=F32)


def _ada_kernel(c_ref, w_ref, b_ref, o_ref):
    c = c_ref[...]
    sh, sl = _split2(c * jax.nn.sigmoid(c))
    wh, wl = _split2(w_ref[...])
    o_ref[...] = (_dot(sh, wh) + (_dot(sl, wh) + _dot(sh, wl))) + b_ref[...]


def _ada(c_pad, w_ada, b_ada):
    n = w_ada.shape[1]
    tn = 1536
    return pl.pallas_call(
        _ada_kernel,
        out_shape=jax.ShapeDtypeStruct((c_pad.shape[0], n), F32),
        grid=(n // tn,),
        in_specs=[pl.BlockSpec(c_pad.shape, lambda j: (0, 0)),
                  pl.BlockSpec((D_MODEL, tn), lambda j: (0, j)),
                  pl.BlockSpec((1, tn), lambda j: (0, j))],
        out_specs=pl.BlockSpec((c_pad.shape[0], tn), lambda j: (0, j)),
        name="ada",
    )(c_pad, w_ada, b_ada)


def _inproj_kernel(x_ref, mod_ref, wqv_t_ref, wk_ref, wf_ref, bf_ref, wa_ref, wg_ref,
                   pselq_t_ref, pselk_ref, wdw_ref, bdw_ref, gng_ref, gnb_ref, gout_ref, gmat_ref,
                   qt_ref, vt_ref, k_ref, augq_t_ref, augk_ref, cn_ref, carry_ref, ext_ref):
    tm = x_ref.shape[1]

    @pl.when(pl.program_id(1) == 0)
    def _():
        carry_ref[...] = jnp.zeros_like(carry_ref)
        ext_ref[0:HALO, :] = jnp.zeros((HALO, CONV_WIDTH), F32)
        ext_ref[HALO + tm:, :] = jnp.zeros((EXT_TAIL, CONV_WIDTH), F32)

    x = x_ref[0]
    shift = mod_ref[0, 0:1, :]
    scale = mod_ref[0, 1:2, :]
    u = (x * (1.0 + scale) + shift).astype(BF16)

    ext_ref[HALO:HALO + tm, :] = _dot(u, wa_ref[...]) * jax.nn.sigmoid(_dot(u, wg_ref[...]))

    def qv_rows(lo, hi):
        return lax.dot_general(wqv_t_ref[lo:hi, :], u, NT_DIMS, preferred_element_type=F32)

    def emit_q(lo, hi):
        qt_ref[0, lo:hi, :] = (qv_rows(lo, hi) * (LOG2E * HEAD_DIM ** -0.5)).astype(BF16)

    def emit_v(lo, hi):
        v_t = qv_rows(ATTN_WIDTH + lo, ATTN_WIDTH + hi).astype(BF16)
        for t in range(tm // TK):
            vt_ref[0, t, lo:hi, :] = v_t[:, t * TK:(t + 1) * TK]

    def emit_k():
        k_ref[0] = _dot(u, wk_ref[...]).astype(BF16)

    def emit_decay():
        fl = _dot(u, wf_ref[...]) + bf_ref[...]
        log_f = jnp.minimum(fl, 0.0) - jnp.log(1.0 + jnp.exp(-jnp.abs(fl)))
        row = lax.broadcasted_iota(jnp.int32, (tm, tm), 0)
        col = lax.broadcasted_iota(jnp.int32, (tm, tm), 1)
        tri = jnp.where(row >= col, 1.0, 0.0).astype(BF16)
        h, m, l = _split3(log_f)
        cum = carry_ref[...] + ((_dot(tri, h) + _dot(tri, m)) + _dot(tri, l))
        carry_ref[...] = cum[tm - 1:tm, :]
        lane = lax.broadcasted_iota(jnp.int32, (tm, LANES), 1)
        ch, cm, cl = (t.astype(F32) for t in _split3(cum * LOG2E))
        pieces = jnp.where(lane < PIECE_LANE[1], ch,
                           jnp.where(lane < PIECE_LANE[2], cm,
                                     jnp.where(lane < ONE_LANE, cl,
                                               jnp.where(lane == ONE_LANE, 1.0, 0.0))))
        pieces = pieces.astype(BF16)
        augk_ref[0] = _dot(pieces, pselk_ref[...]).astype(BF16)
        augq_t_ref[0] = lax.dot_general(pselq_t_ref[...], pieces, NT_DIMS,
                                        preferred_element_type=F32).astype(BF16)

    half = ATTN_WIDTH // 2
    matmul_items = [lambda: emit_q(0, half), lambda: emit_q(half, ATTN_WIDTH),
                    lambda: emit_v(0, half), lambda: emit_v(half, ATTN_WIDTH),
                    emit_k, emit_decay]

    first = HALO - (CONV_KERNEL - 1)
    rows = tm + 2 * SUBLANES
    y = jnp.zeros((tm, CONV_WIDTH), F32) + bdw_ref[...]
    for b in range(SUBLANES):
        zb = None
        for a in range(pl.cdiv(CONV_KERNEL - b, SUBLANES)):
            k = SUBLANES * a + b
            term = wdw_ref[k:k + 1, :] * ext_ref[SUBLANES * a:SUBLANES * a + rows, :]
            zb = term if zb is None else zb + term
        y = y + zb[first + b:first + b + tm, :]
        if b < len(matmul_items):
            matmul_items[b]()
    ext_ref[0:HALO, :] = ext_ref[tm:tm + HALO, :]
    for r in range(0, tm, ROW_GROUP):
        cn_ref[0, r:r + ROW_GROUP, :] = _norm_conv_branch(y[r:r + ROW_GROUP], gng_ref, gnb_ref,
                                                          gout_ref, gmat_ref)


def _norm_conv_branch(y, gng_ref, gnb_ref, gout_ref, gmat_ref):
    gmat = gmat_ref[...]
    yh, yl = _split2(y)
    mu = _dot(yh, gmat) + _dot(yl, gmat)
    d = y - mu
    dh, dl = _split2(d * d)
    var = _dot(dh, gmat) + _dot(dl, gmat)
    yn = d * lax.rsqrt(var + LN_EPS) * gng_ref[...] + gnb_ref[...]
    sw = yn * jax.nn.sigmoid(yn)
    ms = jnp.mean(sw * sw, axis=-1, keepdims=True)
    return (sw * lax.rsqrt(ms + LN_EPS) * gout_ref[...]).astype(BF16)


def _decay_routing_matrices():
    selq_t = np.zeros((N_HEADS * AUG_ROWS, LANES), np.float32)
    selk = np.zeros((LANES, N_PAIRS * LANES), np.float32)
    for head in range(N_HEADS):
        p, j = divmod(head, 2)
        qbase = head * AUG_ROWS + AUG_PER_HEAD * j
        kbase = p * LANES + AUG_PER_HEAD * j
        for i in range(3):
            selq_t[qbase + i, PIECE_LANE[i] + head] = 1.0
            selq_t[qbase + 3 + i, ONE_LANE] = 1.0
            selk[ONE_LANE, kbase + i] = 1.0
            selk[PIECE_LANE[i] + head, kbase + 3 + i] = -1.0
    return jnp.asarray(selq_t, BF16), jnp.asarray(selk, BF16)


def _inproj(x, mod1, wqv_t, wk, wf, bf, wa, wg, pselq_t, pselk, *conv_params):
    b, s, d = x.shape
    tm = TM_PROJ
    const = lambda shape: pl.BlockSpec(shape, lambda bi, i: (0,) * len(shape))
    conv_specs = [const(p.shape) for p in conv_params]
    return pl.pallas_call(
        _inproj_kernel,
        out_shape=(jax.ShapeDtypeStruct((b, ATTN_WIDTH, s), BF16),
                   jax.ShapeDtypeStruct((b, s // TK, ATTN_WIDTH, TK), BF16),
                   jax.ShapeDtypeStruct((b, s, ATTN_WIDTH), BF16),
                   jax.ShapeDtypeStruct((b, N_HEADS * AUG_ROWS, s), BF16),
                   jax.ShapeDtypeStruct((b, s, N_PAIRS * LANES), BF16),
                   jax.ShapeDtypeStruct((b, s, CONV_WIDTH), BF16)),
        grid=(b, s // tm),
        in_specs=[pl.BlockSpec((1, tm, d), lambda bi, i: (bi, i, 0)),
                  pl.BlockSpec((1, 3, d), lambda bi, i: (bi, 0, 0)),
                  const(wqv_t.shape), const(wk.shape), const(wf.shape), const(bf.shape),
                  const(wa.shape), const(wg.shape), const(pselq_t.shape), const(pselk.shape)]
                 + conv_specs,
        out_specs=(pl.BlockSpec((1, ATTN_WIDTH, tm), lambda bi, i: (bi, 0, i)),
                   pl.BlockSpec((1, tm // TK, ATTN_WIDTH, TK), lambda bi, i: (bi, i, 0, 0)),
                   pl.BlockSpec((1, tm, ATTN_WIDTH), lambda bi, i: (bi, i, 0)),
                   pl.BlockSpec((1, N_HEADS * AUG_ROWS, tm), lambda bi, i: (bi, 0, i)),
                   pl.BlockSpec((1, tm, N_PAIRS * LANES), lambda bi, i: (bi, i, 0)),
                   pl.BlockSpec((1, tm, CONV_WIDTH), lambda bi, i: (bi, i, 0))),
        scratch_shapes=[pltpu.VMEM((1, LANES), F32),
                        pltpu.VMEM((HALO + tm + EXT_TAIL, CONV_WIDTH), F32)],
        compiler_params=pltpu.CompilerParams(
            dimension_semantics=("arbitrary", "arbitrary"), vmem_limit_bytes=VMEM_LIMIT),
        name="inproj",
    )(x, mod1, wqv_t, wk, wf, bf, wa, wg, pselq_t, pselk, *conv_params)


def _attn_kernel(qt_ref, aqt_ref, k_ref, ak_ref, vt_ref, o_ref, s_scr, p_scr, acc_scr, st_scr):
    assert TILES_PER_STEP % KV_UNROLL == 0
    for sub in range(TILES_PER_STEP):
        cols = slice(sub * TQ, (sub + 1) * TQ)
        _attn_tile(pl.program_id(2) * TILES_PER_STEP + sub, sub % KV_UNROLL,
                   qt_ref.at[0, :, cols], aqt_ref.at[0, :, cols], k_ref, ak_ref, vt_ref,
                   o_ref.at[0, cols, :], s_scr.at[sub], p_scr.at[sub], acc_scr.at[sub], st_scr.at[sub])


def _attn_tile(qi, rem, qt_ref, aqt_ref, k_ref, ak_ref, vt_ref, o_ref, s_scr, p_scr, acc_scr, st_scr):
    tq = qt_ref.shape[1]
    assert tq == TK, "the drain handles exactly one diagonal key block"
    heads = range(HEADS_PER_STEP)

    zeros_half = jnp.zeros((HEAD_DIM, tq), BF16)
    zeros_tail = jnp.zeros((LANES - AUG_ROWS, tq), BF16)
    qcat_t = []
    for h in heads:
        qh = qt_ref[HEAD_DIM * h:HEAD_DIM * (h + 1), :]
        halves = [qh, zeros_half] if h % 2 == 0 else [zeros_half, qh]
        qcat_t.append(jnp.concatenate(
            halves + [aqt_ref[AUG_ROWS * h:AUG_ROWS * (h + 1), :], zeros_tail], axis=0))

    def logits(kb, h):
        start = pl.multiple_of(kb * TK, TK)
        pair = pl.ds((h // 2) * LANES, LANES)
        kcat = jnp.concatenate([k_ref[0, pl.ds(start, TK), pair], ak_ref[0, pl.ds(start, TK), pair]],
                               axis=-1)
        return _dot(kcat, qcat_t[h])

    def block_max(s):
        return jnp.max(s, axis=0, keepdims=True)

    def softmax_update(s, m_blk, m):
        m_new = jnp.maximum(m, m_blk)
        alpha = jnp.exp2(m - m_new)
        p = jnp.exp2((s - m_new).astype(BF16))
        return p, alpha, m_new

    ones_rows = jnp.ones((AUG_ROWS, TK), BF16)

    def accumulate(kb, h, p, alpha, acc):
        vt = jnp.concatenate([vt_ref[0, kb, HEAD_DIM * h:HEAD_DIM * (h + 1), :], ones_rows], axis=0)
        return alpha * acc + _dot(vt, p)

    def stat(h, j):
        return st_scr.at[pl.ds(3 * h + j, 1), :]

    for h in heads:
        s0 = logits(0, h)
        s_scr[h] = s0
        p_scr[h] = jnp.zeros((TK, tq), BF16)
        acc_scr[h] = jnp.zeros((HEAD_DIM + AUG_ROWS, tq), F32)
        stat(h, 0)[...] = jnp.full((1, tq), NEG_BIG, F32)
        stat(h, 1)[...] = block_max(s0)
        stat(h, 2)[...] = jnp.ones((1, tq), F32)

    def step(kb):
        for h in heads:
            s_next = logits(kb + 1, h)
            p, alpha, m = softmax_update(s_scr[h], stat(h, 1)[...], stat(h, 0)[...])
            acc_scr[h] = accumulate(jnp.maximum(kb - 1, 0), h, p_scr[h], stat(h, 2)[...], acc_scr[h])
            s_scr[h] = s_next
            p_scr[h] = p
            stat(h, 0)[...] = m
            stat(h, 1)[...] = block_max(s_next)
            stat(h, 2)[...] = alpha

    def body(i, carry):
        for u in range(KV_UNROLL):
            step(KV_UNROLL * i + u)
        return carry

    lax.fori_loop(0, (qi - rem) // KV_UNROLL, body, 0)
    for u in range(rem):
        step(qi - rem + u)

    key = lax.broadcasted_iota(jnp.int32, (TK, tq), 0)
    qry = lax.broadcasted_iota(jnp.int32, (TK, tq), 1)
    outs = []
    for h in heads:
        s = jnp.where(key <= qry, s_scr[h], NEG_BIG)
        p, alpha, _ = softmax_update(s, block_max(s), stat(h, 0)[...])
        acc = accumulate(jnp.maximum(qi - 1, 0), h, p_scr[h], stat(h, 2)[...], acc_scr[h])
        acc = accumulate(qi, h, p, alpha, acc)
        outs.append(acc[:HEAD_DIM] / acc[HEAD_DIM:HEAD_DIM + 1])
    for pr in range(HEADS_PER_STEP // 2):
        o_ref[:, pr * LANES:(pr + 1) * LANES] = jnp.concatenate(outs[2 * pr:2 * pr + 2], axis=0).T


def _attn(qt, augq_t, k, augk, vt):
    b, s, _ = k.shape
    g, n = HEADS_PER_STEP, TILES_PER_STEP
    tq_step = n * TQ
    return pl.pallas_call(
        _attn_kernel,
        out_shape=jax.ShapeDtypeStruct((b, s, ATTN_WIDTH), F32),
        grid=(b, N_HEADS // g, s // tq_step),
        in_specs=[pl.BlockSpec((1, g * HEAD_DIM, tq_step), lambda bi, p, i: (bi, p, i)),
                  pl.BlockSpec((1, g * AUG_ROWS, tq_step), lambda bi, p, i: (bi, p, i)),
                  pl.BlockSpec((1, s, g * HEAD_DIM), lambda bi, p, i: (bi, 0, p),
                               pipeline_mode=pl.Buffered(1)),
                  pl.BlockSpec((1, s, g * HEAD_DIM), lambda bi, p, i: (bi, 0, p),
                               pipeline_mode=pl.Buffered(1)),
                  pl.BlockSpec((1, s // TK, g * HEAD_DIM, TK), lambda bi, p, i: (bi, 0, p, 0),
                               pipeline_mode=pl.Buffered(1))],
        out_specs=pl.BlockSpec((1, tq_step, g * HEAD_DIM), lambda bi, p, i: (bi, i, p)),
        scratch_shapes=[pltpu.VMEM((n, g, TK, TQ), F32), pltpu.VMEM((n, g, TK, TQ), BF16),
                        pltpu.VMEM((n, g, HEAD_DIM + AUG_ROWS, TQ), F32),
                        pltpu.VMEM((n, 3 * g, TQ), F32)],
        compiler_params=pltpu.CompilerParams(
            dimension_semantics=("arbitrary", "arbitrary", "arbitrary"),
            vmem_limit_bytes=VMEM_LIMIT),
        name="attn",
    )(qt, augq_t, k, augk, vt)


def _layernorm(y, g, b):
    mu = jnp.mean(y, axis=-1, keepdims=True)
    d = y - mu
    var = jnp.mean(d * d, axis=-1, keepdims=True)
    return d * lax.rsqrt(var + LN_EPS) * g + b


def _mix_mlp_kernel(attn_ref, cn_ref, x_ref, mod_ref, gattn_ref, wo_a_ref, wo_c_ref,
                    ln1g_ref, ln1b_ref, w1_ref, w2_ref, ln2g_ref, ln2b_ref, o_ref):
    tm = x_ref.shape[1]
    groups = [slice(r, r + ROW_GROUP) for r in range(0, tm, ROW_GROUP)]
    gate1, shift2, scale2, gate2 = (mod_ref[0, j:j + 1, :] for j in range(2, 6))

    def token_mix(rows):
        a = attn_ref[0, rows, :]
        ms = jnp.mean(a * a, axis=-1, keepdims=True)
        an = (a * lax.rsqrt(ms + LN_EPS) * gattn_ref[...]).astype(BF16)
        mixed = _dot(an, wo_a_ref[...]) + _dot(cn_ref[0, rows, :], wo_c_ref[...])
        return DEEPNORM_ALPHA * x_ref[0, rows, :] + (1.0 + gate1) * mixed

    def mlp(x1):
        u = (x1 * (1.0 + scale2) + shift2).astype(BF16)
        ff = jnp.zeros(x1.shape, F32)
        for c in range(0, D_FF, FF_CHUNK):
            hid = jnp.maximum(_dot(u, w1_ref[:, c:c + FF_CHUNK]), 0.0)
            ff = ff + _dot((hid * hid).astype(BF16), w2_ref[c:c + FF_CHUNK, :])
        return DEEPNORM_ALPHA * x1 + (1.0 + gate2) * ff

    x1 = [_layernorm(token_mix(rows), ln1g_ref[...], ln1b_ref[...]) for rows in groups]
    for rows, x1_rows in zip(groups, x1):
        o_ref[0, rows, :] = _layernorm(mlp(x1_rows), ln2g_ref[...], ln2b_ref[...])


def _mix_mlp(attn, cn, x, mods, gattn, wo_a, wo_c, ln1g, ln1b, w1, w2, ln2g, ln2b):
    b, s, d = x.shape
    tm = TM_PROJ
    const = lambda a: pl.BlockSpec(a.shape, lambda bi, i: (0,) * a.ndim, pipeline_mode=pl.Buffered(1))
    return pl.pallas_call(
        _mix_mlp_kernel,
        out_shape=jax.ShapeDtypeStruct((b, s, d), F32),
        grid=(b, s // tm),
        in_specs=[pl.BlockSpec((1, tm, ATTN_WIDTH), lambda bi, i: (bi, i, 0)),
                  pl.BlockSpec((1, tm, CONV_WIDTH), lambda bi, i: (bi, i, 0)),
                  pl.BlockSpec((1, tm, d), lambda bi, i: (bi, i, 0)),
                  pl.BlockSpec((1, 6, d), lambda bi, i: (bi, 0, 0)),
                  const(gattn), const(wo_a), const(wo_c), const(ln1g), const(ln1b),
                  const(w1), const(w2), const(ln2g), const(ln2b)],
        out_specs=pl.BlockSpec((1, tm, d), lambda bi, i: (bi, i, 0)),
        compiler_params=pltpu.CompilerParams(
            dimension_semantics=("arbitrary", "arbitrary"), vmem_limit_bytes=VMEM_LIMIT),
        name="mix_mlp",
    )(attn, cn, x, mods, gattn, wo_a, wo_c, ln1g, ln1b, w1, w2, ln2g, ln2b)


def kernel(x, c, w_ada, b_ada, w_in, b_forget, w_dw, b_dw, gn_g, gn_b, g_attn_out, g_conv_out,
           w_out, ln1_g, ln1_b, w_ff1, w_ff2, ln2_g, ln2_b):
    bsz = x.shape[0]
    layer = 0
    row = lambda v: v.reshape(1, -1)

    c_pad = jnp.pad(c, ((0, 2 * SUBLANES - bsz), (0, 0)))
    ada = _ada(c_pad, w_ada[layer], row(b_ada[layer]))[:bsz]
    mods = ada.reshape(bsz, 6, D_MODEL)
    mod1 = mods[:, 0:3]

    w = w_in[layer]
    a0 = 3 * ATTN_WIDTH + N_HEADS
    wq, wk, wv = (w[:, i * ATTN_WIDTH:(i + 1) * ATTN_WIDTH] for i in range(3))
    wqv_t = jnp.concatenate([wq, wv], axis=1).T.astype(BF16)
    wk = wk.astype(BF16)
    reps = LANES // N_HEADS
    wf = jnp.tile(w[:, 3 * ATTN_WIDTH:a0], (1, reps)).astype(BF16)
    bf = jnp.tile(b_forget[layer], reps).reshape(1, LANES)
    wa = w[:, a0:a0 + CONV_WIDTH].astype(BF16)
    wg = w[:, a0 + CONV_WIDTH:].astype(BF16)

    pselq_t, pselk = _decay_routing_matrices()
    grp = np.arange(CONV_WIDTH) // CONV_GROUP
    gmat = jnp.asarray((grp[:, None] == grp[None, :]).astype(np.float32) / CONV_GROUP, BF16)
    qt, vt, k, augq_t, augk, cn = _inproj(
        x, mod1, wqv_t, wk, wf, bf, wa, wg, pselq_t, pselk,
        w_dw[layer].reshape(CONV_KERNEL, CONV_WIDTH), row(b_dw[layer]),
        row(gn_g[layer]), row(gn_b[layer]), row(g_conv_out[layer]), gmat)

    attn = _attn(qt, augq_t, k, augk, vt)

    wo = w_out[layer].astype(BF16)
    return _mix_mlp(attn, cn, x, mods, row(g_attn_out[layer]), wo[:ATTN_WIDTH], wo[ATTN_WIDTH:],
                    row(ln1_g[layer]), row(ln1_b[layer]),
                    w_ff1[layer].astype(BF16), w_ff2[layer].astype(BF16),
                    row(ln2_g[layer]), row(ln2_b[layer]))
```

```python
import numpy as np
import jax
import jax.numpy as jnp
from jax import lax
from jax.experimental import pallas as pl
from jax.experimental.pallas import tpu as pltpu

D_MODEL = 1024
HEAD_DIM = 64
ATTN_WIDTH = 512
CONV_WIDTH = 512
N_HEADS = 8
N_PAIRS = N_HEADS // 2
CONV_KERNEL = 31
CONV_GROUP = 64
D_FF = 4 * D_MODEL
LN_EPS = 1e-5
DEEPNORM_ALPHA = 2.0 ** 0.25

LANES = 128
SUBLANES = 8
HALO = 32
EXT_TAIL = 8
AUG_PER_HEAD = 6
AUG_ROWS = 16
NT_DIMS = (((1,), (1,)), ((), ()))
PIECE_LANE = (0, 8, 16)
ONE_LANE = 24
NEG_BIG = -1e30

TM_PROJ = 512
TM_MLP = 1024
ROW_GROUP = 256
FF_CHUNK = 1024
TQ = 256
TK = 256
HEADS_PER_STEP = 8
KV_UNROLLS = (4, 2, 1)
TILES_PER_STEP = 2
LOG2E = 1.4426950408889634
VMEM_LIMIT = 56 * 1024 * 1024

F32 = jnp.float32
BF16 = jnp.bfloat16


def _split3(x):
    hi = x.astype(BF16)
    r = x - hi.astype(F32)
    mid = r.astype(BF16)
    lo = (r - mid.astype(F32)).astype(BF16)
    return hi, mid, lo


def _split2(x):
    hi = x.astype(BF16)
    lo = (x - hi.astype(F32)).astype(BF16)
    return hi, lo


def _dot(a, b):
    return jnp.dot(a, b, preferred_element_type=F32)


def _ada_kernel(c_ref, w_ref, b_ref, o_ref):
    c = c_ref[...]
    sh, sl = _split2(c * jax.nn.sigmoid(c))
    wh, wl = _split2(w_ref[...])
    o_ref[...] = (_dot(sh, wh) + (_dot(sl, wh) + _dot(sh, wl))) + b_ref[...]


def _ada(c_pad, w_ada, b_ada):
    n = w_ada.shape[1]
    tn = 1536
    return pl.pallas_call(
        _ada_kernel,
        out_shape=jax.ShapeDtypeStruct((c_pad.shape[0], n), F32),
        grid=(n // tn,),
        in_specs=[pl.BlockSpec(c_pad.shape, lambda j: (0, 0)),
                  pl.BlockSpec((D_MODEL, tn), lambda j: (0, j)),
                  pl.BlockSpec((1, tn), lambda j: (0, j))],
        out_specs=pl.BlockSpec((c_pad.shape[0], tn), lambda j: (0, j)),
        name="ada",
    )(c_pad, w_ada, b_ada)


def _inproj_kernel(x_ref, mod_ref, wqv_t_ref, wk_ref, wf_ref, bf_ref, wa_ref, wg_ref,
                   pselq_t_ref, pselk_ref, wdw_ref, bdw_ref, gng_ref, gnb_ref, gout_ref, gmat_ref,
                   qt_ref, vt_ref, k_ref, augq_t_ref, augk_ref, cn_ref, carry_ref, ext_ref):
    tm = x_ref.shape[1]

    @pl.when(pl.program_id(1) == 0)
    def _():
        carry_ref[...] = jnp.zeros_like(carry_ref)
        ext_ref[0:HALO, :] = jnp.zeros((HALO, CONV_WIDTH), F32)
        ext_ref[HALO + tm:, :] = jnp.zeros((EXT_TAIL, CONV_WIDTH), F32)

    x = x_ref[0]
    shift = mod_ref[0, 0:1, :]
    scale = mod_ref[0, 1:2, :]
    u = (x * (1.0 + scale) + shift).astype(BF16)

    ext_ref[HALO:HALO + tm, :] = _dot(u, wa_ref[...]) * jax.nn.sigmoid(_dot(u, wg_ref[...]))

    def qv_rows(lo, hi):
        return lax.dot_general(wqv_t_ref[lo:hi, :], u, NT_DIMS, preferred_element_type=F32)

    def emit_q(lo, hi):
        qt_ref[0, lo:hi, :] = (qv_rows(lo, hi) * (LOG2E * HEAD_DIM ** -0.5)).astype(BF16)

    def emit_v(lo, hi):
        v_t = qv_rows(ATTN_WIDTH + lo, ATTN_WIDTH + hi).astype(BF16)
        for t in range(tm // TK):
            vt_ref[0, t, lo:hi, :] = v_t[:, t * TK:(t + 1) * TK]

    def emit_k():
        k_ref[0] = _dot(u, wk_ref[...]).astype(BF16)

    def emit_decay():
        fl = _dot(u, wf_ref[...]) + bf_ref[...]
        log_f = jnp.minimum(fl, 0.0) - jnp.log(1.0 + jnp.exp(-jnp.abs(fl)))
        row = lax.broadcasted_iota(jnp.int32, (tm, tm), 0)
        col = lax.broadcasted_iota(jnp.int32, (tm, tm), 1)
        tri = jnp.where(row >= col, 1.0, 0.0).astype(BF16)
        h, m, l = _split3(log_f)
        cum = carry_ref[...] + ((_dot(tri, h) + _dot(tri, m)) + _dot(tri, l))
        carry_ref[...] = cum[tm - 1:tm, :]
        lane = lax.broadcasted_iota(jnp.int32, (tm, LANES), 1)
        ch, cm, cl = (t.astype(F32) for t in _split3(cum * LOG2E))
        pieces = jnp.where(lane < PIECE_LANE[1], ch,
                           jnp.where(lane < PIECE_LANE[2], cm,
                                     jnp.where(lane < ONE_LANE, cl,
                                               jnp.where(lane == ONE_LANE, 1.0, 0.0))))
        pieces = pieces.astype(BF16)
        augk_ref[0] = _dot(pieces, pselk_ref[...]).astype(BF16)
        augq_t_ref[0] = lax.dot_general(pselq_t_ref[...], pieces, NT_DIMS,
                                        preferred_element_type=F32).astype(BF16)

    half = ATTN_WIDTH // 2
    matmul_items = [lambda: emit_q(0, half), lambda: emit_q(half, ATTN_WIDTH),
                    lambda: emit_v(0, half), lambda: emit_v(half, ATTN_WIDTH),
                    emit_k, emit_decay]

    first = HALO - (CONV_KERNEL - 1)
    rows = tm + 2 * SUBLANES
    y = jnp.zeros((tm, CONV_WIDTH), F32) + bdw_ref[...]
    for b in range(SUBLANES):
        zb = None
        for a in range(pl.cdiv(CONV_KERNEL - b, SUBLANES)):
            k = SUBLANES * a + b
            term = wdw_ref[k:k + 1, :] * ext_ref[SUBLANES * a:SUBLANES * a + rows, :]
            zb = term if zb is None else zb + term
        y = y + zb[first + b:first + b + tm, :]
        if b < len(matmul_items):
            matmul_items[b]()
    ext_ref[0:HALO, :] = ext_ref[tm:tm + HALO, :]
    for r in range(0, tm, ROW_GROUP):
        cn_ref[0, r:r + ROW_GROUP, :] = _norm_conv_branch(y[r:r + ROW_GROUP], gng_ref, gnb_ref,
                                                          gout_ref, gmat_ref)


def _norm_conv_branch(y, gng_ref, gnb_ref, gout_ref, gmat_ref):
    gmat = gmat_ref[...]
    yh, yl = _split2(y)
    mu = _dot(yh, gmat) + _dot(yl, gmat)
    d = y - mu
    dh, dl = _split2(d * d)
    var = _dot(dh, gmat) + _dot(dl, gmat)
    yn = d * lax.rsqrt(var + LN_EPS) * gng_ref[...] + gnb_ref[...]
    sw = yn * jax.nn.sigmoid(yn)
    ms = jnp.mean(sw * sw, axis=-1, keepdims=True)
    return (sw * lax.rsqrt(ms + LN_EPS) * gout_ref[...]).astype(BF16)


def _decay_routing_matrices():
    selq_t = np.zeros((N_HEADS * AUG_ROWS, LANES), np.float32)
    selk = np.zeros((LANES, N_PAIRS * LANES), np.float32)
    for head in range(N_HEADS):
        p, j = divmod(head, 2)
        qbase = head * AUG_ROWS + AUG_PER_HEAD * j
        kbase = p * LANES + AUG_PER_HEAD * j
        for i in range(3):
            selq_t[qbase + i, PIECE_LANE[i] + head] = 1.0
            selq_t[qbase + 3 + i, ONE_LANE] = 1.0
            selk[ONE_LANE, kbase + i] = 1.0
            selk[PIECE_LANE[i] + head, kbase + 3 + i] = -1.0
    return jnp.asarray(selq_t, BF16), jnp.asarray(selk, BF16)


def _inproj(x, mod1, wqv_t, wk, wf, bf, wa, wg, pselq_t, pselk, *conv_params):
    b, s, d = x.shape
    tm = TM_PROJ
    const = lambda shape: pl.BlockSpec(shape, lambda bi, i: (0,) * len(shape))
    conv_specs = [const(p.shape) for p in conv_params]
    return pl.pallas_call(
        _inproj_kernel,
        out_shape=(jax.ShapeDtypeStruct((b, ATTN_WIDTH, s), BF16),
                   jax.ShapeDtypeStruct((b, s // TK, ATTN_WIDTH, TK), BF16),
                   jax.ShapeDtypeStruct((b, s, ATTN_WIDTH), BF16),
                   jax.ShapeDtypeStruct((b, N_HEADS * AUG_ROWS, s), BF16),
                   jax.ShapeDtypeStruct((b, s, N_PAIRS * LANES), BF16),
                   jax.ShapeDtypeStruct((b, s, CONV_WIDTH), BF16)),
        grid=(b, s // tm),
        in_specs=[pl.BlockSpec((1, tm, d), lambda bi, i: (bi, i, 0)),
                  pl.BlockSpec((1, 3, d), lambda bi, i: (bi, 0, 0)),
                  const(wqv_t.shape), const(wk.shape), const(wf.shape), const(bf.shape),
                  const(wa.shape), const(wg.shape), const(pselq_t.shape), const(pselk.shape)]
                 + conv_specs,
        out_specs=(pl.BlockSpec((1, ATTN_WIDTH, tm), lambda bi, i: (bi, 0, i)),
                   pl.BlockSpec((1, tm // TK, ATTN_WIDTH, TK), lambda bi, i: (bi, i, 0, 0)),
                   pl.BlockSpec((1, tm, ATTN_WIDTH), lambda bi, i: (bi, i, 0)),
                   pl.BlockSpec((1, N_HEADS * AUG_ROWS, tm), lambda bi, i: (bi, 0, i)),
                   pl.BlockSpec((1, tm, N_PAIRS * LANES), lambda bi, i: (bi, i, 0)),
                   pl.BlockSpec((1, tm, CONV_WIDTH), lambda bi, i: (bi, i, 0))),
        scratch_shapes=[pltpu.VMEM((1, LANES), F32),
                        pltpu.VMEM((HALO + tm + EXT_TAIL, CONV_WIDTH), F32)],
        compiler_params=pltpu.CompilerParams(
            dimension_semantics=("arbitrary", "arbitrary"), vmem_limit_bytes=VMEM_LIMIT),
        name="inproj",
    )(x, mod1, wqv_t, wk, wf, bf, wa, wg, pselq_t, pselk, *conv_params)


def _attn_kernel(qt_ref, aqt_ref, k_ref, ak_ref, vt_ref, o_ref, s_scr, p_scr, acc_scr, st_scr):
    for sub in range(TILES_PER_STEP):
        cols = slice(sub * TQ, (sub + 1) * TQ)
        _attn_tile(pl.program_id(2) * TILES_PER_STEP + sub,
                   qt_ref.at[0, :, cols], aqt_ref.at[0, :, cols], k_ref, ak_ref, vt_ref,
                   o_ref.at[0, cols, :], s_scr.at[sub], p_scr.at[sub], acc_scr.at[sub], st_scr.at[sub])


def _attn_tile(qi, qt_ref, aqt_ref, k_ref, ak_ref, vt_ref, o_ref, s_scr, p_scr, acc_scr, st_scr):
    tq = qt_ref.shape[1]
    assert tq == TK, "the drain handles exactly one diagonal key block"
    heads = range(HEADS_PER_STEP)

    zeros_half = jnp.zeros((HEAD_DIM, tq), BF16)
    qcat_t = []
    for h in heads:
        qh = qt_ref[HEAD_DIM * h:HEAD_DIM * (h + 1), :]
        halves = [qh, zeros_half] if h % 2 == 0 else [zeros_half, qh]
        qcat_t.append(jnp.concatenate(halves + [aqt_ref[AUG_ROWS * h:AUG_ROWS * (h + 1), :]], axis=0))

    def logits(kb, h):
        start = pl.multiple_of(kb * TK, TK)
        lane0 = (h // 2) * LANES
        kcat = jnp.concatenate([k_ref[0, pl.ds(start, TK), lane0:lane0 + LANES],
                                ak_ref[0, pl.ds(start, TK), lane0:lane0 + AUG_ROWS]], axis=-1)
        return _dot(kcat, qcat_t[h])

    def block_max(s):
        return jnp.max(s, axis=0, keepdims=True)

    def softmax_update(s, m_blk, m):
        m_new = jnp.maximum(m, m_blk)
        alpha = jnp.exp2(m - m_new)
        p = jnp.exp2((s - m_new).astype(BF16))
        return p, alpha, m_new

    ones_rows = jnp.ones((AUG_ROWS, TK), BF16)

    def accumulate(kb, h, p, alpha, acc):
        vt = jnp.concatenate([vt_ref[0, kb, HEAD_DIM * h:HEAD_DIM * (h + 1), :], ones_rows], axis=0)
        return alpha * acc + _dot(vt, p)

    def stat(h, j):
        return st_scr.at[pl.ds(3 * h + j, 1), :]

    for h in heads:
        s0 = logits(0, h)
        s_scr[h] = s0
        p_scr[h] = jnp.zeros((TK, tq), BF16)
        acc_scr[h] = jnp.zeros((HEAD_DIM + AUG_ROWS, tq), F32)
        stat(h, 0)[...] = jnp.full((1, tq), NEG_BIG, F32)
        stat(h, 1)[...] = block_max(s0)
        stat(h, 2)[...] = jnp.ones((1, tq), F32)

    def step(kb):
        for h in heads:
            s_next = logits(kb + 1, h)
            p, alpha, m = softmax_update(s_scr[h], stat(h, 1)[...], stat(h, 0)[...])
            acc_scr[h] = accumulate(jnp.maximum(kb - 1, 0), h, p_scr[h], stat(h, 2)[...], acc_scr[h])
            s_scr[h] = s_next
            p_scr[h] = p
            stat(h, 0)[...] = m
            stat(h, 1)[...] = block_max(s_next)
            stat(h, 2)[...] = alpha

    done = 0
    for unroll in KV_UNROLLS:
        trips = (qi - done) // unroll

        def body(i, carry, unroll=unroll, base=done):
            for u in range(unroll):
                step(base + unroll * i + u)
            return carry

        lax.fori_loop(0, trips, body, 0)
        done = done + trips * unroll

    key = lax.broadcasted_iota(jnp.int32, (TK, tq), 0)
    qry = lax.broadcasted_iota(jnp.int32, (TK, tq), 1)
    outs = []
    for h in heads:
        s = jnp.where(key <= qry, s_scr[h], NEG_BIG)
        p, alpha, _ = softmax_update(s, block_max(s), stat(h, 0)[...])
        acc = accumulate(jnp.maximum(qi - 1, 0), h, p_scr[h], stat(h, 2)[...], acc_scr[h])
        acc = accumulate(qi, h, p, alpha, acc)
        outs.append(acc[:HEAD_DIM] / acc[HEAD_DIM:HEAD_DIM + 1])
    for pr in range(HEADS_PER_STEP // 2):
        o_ref[:, pr * LANES:(pr + 1) * LANES] = jnp.concatenate(outs[2 * pr:2 * pr + 2], axis=0).T


def _attn(qt, augq_t, k, augk, vt):
    b, s, _ = k.shape
    g, n = HEADS_PER_STEP, TILES_PER_STEP
    tq_step = n * TQ
    return pl.pallas_call(
        _attn_kernel,
        out_shape=jax.ShapeDtypeStruct((b, s, ATTN_WIDTH), F32),
        grid=(b, N_HEADS // g, s // tq_step),
        in_specs=[pl.BlockSpec((1, g * HEAD_DIM, tq_step), lambda bi, p, i: (bi, p, i)),
                  pl.BlockSpec((1, g * AUG_ROWS, tq_step), lambda bi, p, i: (bi, p, i)),
                  pl.BlockSpec((1, s, g * HEAD_DIM), lambda bi, p, i: (bi, 0, p),
                               pipeline_mode=pl.Buffered(1)),
                  pl.BlockSpec((1, s, g * HEAD_DIM), lambda bi, p, i: (bi, 0, p),
                               pipeline_mode=pl.Buffered(1)),
                  pl.BlockSpec((1, s // TK, g * HEAD_DIM, TK), lambda bi, p, i: (bi, 0, p, 0),
                               pipeline_mode=pl.Buffered(1))],
        out_specs=pl.BlockSpec((1, tq_step, g * HEAD_DIM), lambda bi, p, i: (bi, i, p)),
        scratch_shapes=[pltpu.VMEM((n, g, TK, TQ), F32), pltpu.VMEM((n, g, TK, TQ), BF16),
                        pltpu.VMEM((n, g, HEAD_DIM + AUG_ROWS, TQ), F32),
                        pltpu.VMEM((n, 3 * g, TQ), F32)],
        compiler_params=pltpu.CompilerParams(
            dimension_semantics=("arbitrary", "arbitrary", "arbitrary"),
            vmem_limit_bytes=VMEM_LIMIT),
        name="attn",
    )(qt, augq_t, k, augk, vt)


def _layernorm(y, g, b):
    mu = jnp.mean(y, axis=-1, keepdims=True)
    d = y - mu
    var = jnp.mean(d * d, axis=-1, keepdims=True)
    return d * lax.rsqrt(var + LN_EPS) * g + b


def _mix_mlp_kernel(attn_ref, cn_ref, x_ref, mod_ref, gattn_ref, wo_a_ref, wo_c_ref,
                    ln1g_ref, ln1b_ref, w1_ref, w2_ref, ln2g_ref, ln2b_ref, o_ref):
    tm = x_ref.shape[1]
    groups = [slice(r, r + ROW_GROUP) for r in range(0, tm, ROW_GROUP)]
    gate1, shift2, scale2, gate2 = (mod_ref[0, j:j + 1, :] for j in range(2, 6))

    def token_mix(rows):
        a = attn_ref[0, rows, :]
        ms = jnp.mean(a * a, axis=-1, keepdims=True)
        an = (a * lax.rsqrt(ms + LN_EPS) * gattn_ref[...]).astype(BF16)
        mixed = _dot(an, wo_a_ref[...]) + _dot(cn_ref[0, rows, :], wo_c_ref[...])
        return DEEPNORM_ALPHA * x_ref[0, rows, :] + (1.0 + gate1) * mixed

    def mlp(x1):
        u = (x1 * (1.0 + scale2) + shift2).astype(BF16)
        ff = jnp.zeros(x1.shape, F32)
        for c in range(0, D_FF, FF_CHUNK):
            hid = jnp.maximum(_dot(u, w1_ref[:, c:c + FF_CHUNK]), 0.0)
            ff = ff + _dot((hid * hid).astype(BF16), w2_ref[c:c + FF_CHUNK, :])
        return DEEPNORM_ALPHA * x1 + (1.0 + gate2) * ff

    x1 = [_layernorm(token_mix(rows), ln1g_ref[...], ln1b_ref[...]) for rows in groups]
    for rows, x1_rows in zip(groups, x1):
        o_ref[0, rows, :] = _layernorm(mlp(x1_rows), ln2g_ref[...], ln2b_ref[...])


def _mix_mlp(attn, cn, x, mods, gattn, wo_a, wo_c, ln1g, ln1b, w1, w2, ln2g, ln2b):
    b, s, d = x.shape
    tm = TM_MLP
    const = lambda a: pl.BlockSpec(a.shape, lambda bi, i: (0,) * a.ndim, pipeline_mode=pl.Buffered(1))
    return pl.pallas_call(
        _mix_mlp_kernel,
        out_shape=jax.ShapeDtypeStruct((b, s, d), F32),
        grid=(b, s // tm),
        in_specs=[pl.BlockSpec((1, tm, ATTN_WIDTH), lambda bi, i: (bi, i, 0)),
                  pl.BlockSpec((1, tm, CONV_WIDTH), lambda bi, i: (bi, i, 0)),
                  pl.BlockSpec((1, tm, d), lambda bi, i: (bi, i, 0)),
                  pl.BlockSpec((1, 6, d), lambda bi, i: (bi, 0, 0)),
                  const(gattn), const(wo_a), const(wo_c), const(ln1g), const(ln1b),
                  const(w1), const(w2), const(ln2g), const(ln2b)],
        out_specs=pl.BlockSpec((1, tm, d), lambda bi, i: (bi, i, 0)),
        compiler_params=pltpu.CompilerParams(
            dimension_semantics=("arbitrary", "arbitrary"), vmem_limit_bytes=VMEM_LIMIT),
        name="mix_mlp",
    )(attn, cn, x, mods, gattn, wo_a, wo_c, ln1g, ln1b, w1, w2, ln2g, ln2b)


def kernel(x, c, w_ada, b_ada, w_in, b_forget, w_dw, b_dw, gn_g, gn_b, g_attn_out, g_conv_out,
           w_out, ln1_g, ln1_b, w_ff1, w_ff2, ln2_g, ln2_b):
    bsz = x.shape[0]
    layer = 0
    row = lambda v: v.reshape(1, -1)

    c_pad = jnp.pad(c, ((0, 2 * SUBLANES - bsz), (0, 0)))
    ada = _ada(c_pad, w_ada[layer], row(b_ada[layer]))[:bsz]
    mods = ada.reshape(bsz, 6, D_MODEL)
    mod1 = mods[:, 0:3]

    w = w_in[layer]
    a0 = 3 * ATTN_WIDTH + N_HEADS
    wq, wk, wv = (w[:, i * ATTN_WIDTH:(i + 1) * ATTN_WIDTH] for i in range(3))
    wqv_t = jnp.concatenate([wq, wv], axis=1).T.astype(BF16)
    wk = wk.astype(BF16)
    reps = LANES // N_HEADS
    wf = jnp.tile(w[:, 3 * ATTN_WIDTH:a0], (1, reps)).astype(BF16)
    bf = jnp.tile(b_forget[layer], reps).reshape(1, LANES)
    wa = w[:, a0:a0 + CONV_WIDTH].astype(BF16)
    wg = w[:, a0 + CONV_WIDTH:].astype(BF16)

    pselq_t, pselk = _decay_routing_matrices()
    grp = np.arange(CONV_WIDTH) // CONV_GROUP
    gmat = jnp.asarray((grp[:, None] == grp[None, :]).astype(np.float32) / CONV_GROUP, BF16)
    qt, vt, k, augq_t, augk, cn = _inproj(
        x, mod1, wqv_t, wk, wf, bf, wa, wg, pselq_t, pselk,
        w_dw[layer].reshape(CONV_KERNEL, CONV_WIDTH), row(b_dw[layer]),
        row(gn_g[layer]), row(gn_b[layer]), row(g_conv_out[layer]), gmat)

    attn = _attn(qt, augq_t, k, augk, vt)

    wo = w_out[layer].astype(BF16)
    return _mix_mlp(attn, cn, x, mods, row(g_attn_out[layer]), wo[:ATTN_WIDTH], wo[ATTN_WIDTH:],
                    row(ln1_g[layer]), row(ln1_b[layer]),
                    w_ff1[layer].astype(BF16), w_ff2[layer].astype(BF16),
                    row(ln2_g[layer]), row(ln2_b[layer]))
```

```python
import functools

import numpy as np
import jax
import jax.numpy as jnp
from jax import lax
from jax.experimental import pallas as pl
from jax.experimental.pallas import tpu as pltpu

D_MODEL = 1024
HEAD_DIM = 64
ATTN_WIDTH = 512
CONV_WIDTH = 512
N_HEADS = 8
N_PAIRS = N_HEADS // 2
CONV_KERNEL = 31
CONV_GROUP = 64
D_FF = 4 * D_MODEL
LN_EPS = 1e-5
DEEPNORM_ALPHA = 2.0 ** 0.25

LANES = 128
SUBLANES = 8
HALO = 32
EXT_TAIL = 8
AUG_PER_HEAD = 6
AUG_ROWS = 16
NT_DIMS = (((1,), (1,)), ((), ()))
PIECE_LANE = (0, 8, 16)
ONE_LANE = 24
NEG_BIG = -1e30

TM_PROJ = 512
ROW_GROUP = 256
FF_CHUNK = 1024
TQ = 256
TK = 256
HEADS_PER_STEP = 8
KV_UNROLLS = (4, 2, 1)
TILES_PER_STEP = 2
LOG2E = 1.4426950408889634
VMEM_LIMIT = 56 * 1024 * 1024

F32 = jnp.float32
BF16 = jnp.bfloat16


def _split3(x):
    hi = x.astype(BF16)
    r = x - hi.astype(F32)
    mid = r.astype(BF16)
    lo = (r - mid.astype(F32)).astype(BF16)
    return hi, mid, lo


def _split2(x):
    hi = x.astype(BF16)
    lo = (x - hi.astype(F32)).astype(BF16)
    return hi, lo


def _dot(a, b):
    return jnp.dot(a, b, preferred_element_type=F32)


def _ada_kernel(c_ref, w_ref, b_ref, o_ref):
    c = c_ref[...]
    sh, sl = _split2(c * jax.nn.sigmoid(c))
    wh, wl = _split2(w_ref[...])
    o_ref[...] = (_dot(sh, wh) + (_dot(sl, wh) + _dot(sh, wl))) + b_ref[...]


def _ada(c_pad, w_ada, b_ada):
    n = w_ada.shape[1]
    tn = 1536
    return pl.pallas_call(
        _ada_kernel,
        out_shape=jax.ShapeDtypeStruct((c_pad.shape[0], n), F32),
        grid=(n // tn,),
        in_specs=[pl.BlockSpec(c_pad.shape, lambda j: (0, 0)),
                  pl.BlockSpec((D_MODEL, tn), lambda j: (0, j)),
                  pl.BlockSpec((1, tn), lambda j: (0, j))],
        out_specs=pl.BlockSpec((c_pad.shape[0], tn), lambda j: (0, j)),
        name="ada",
    )(c_pad, w_ada, b_ada)


def _inproj_kernel(x_ref, mod_ref, wqv_t_ref, wk_ref, wf_ref, bf_ref, wa_ref, wg_ref,
                   pselq_t_ref, pselk_ref, wdw_ref, bdw_ref, gng_ref, gnb_ref, gout_ref, gmat_ref,
                   qt_ref, vt_ref, k_ref, augq_t_ref, augk_ref, cn_ref, carry_ref, ext_ref):
    tm = x_ref.shape[1]

    @pl.when(pl.program_id(1) == 0)
    def _():
        carry_ref[...] = jnp.zeros_like(carry_ref)
        ext_ref[0:HALO, :] = jnp.zeros((HALO, CONV_WIDTH), F32)
        ext_ref[HALO + tm:, :] = jnp.zeros((EXT_TAIL, CONV_WIDTH), F32)

    x = x_ref[0]
    shift = mod_ref[0, 0:1, :]
    scale = mod_ref[0, 1:2, :]
    u = (x * (1.0 + scale) + shift).astype(BF16)

    ext_ref[HALO:HALO + tm, :] = _dot(u, wa_ref[...]) * jax.nn.sigmoid(_dot(u, wg_ref[...]))

    def qv_rows(lo, hi):
        return lax.dot_general(wqv_t_ref[lo:hi, :], u, NT_DIMS, preferred_element_type=F32)

    def emit_q(lo, hi):
        qt_ref[0, lo:hi, :] = (qv_rows(lo, hi) * (LOG2E * HEAD_DIM ** -0.5)).astype(BF16)

    def emit_v(lo, hi):
        v_t = qv_rows(ATTN_WIDTH + lo, ATTN_WIDTH + hi).astype(BF16)
        for t in range(tm // TK):
            vt_ref[0, t, lo:hi, :] = v_t[:, t * TK:(t + 1) * TK]

    def emit_k():
        k_ref[0] = _dot(u, wk_ref[...]).astype(BF16)

    def emit_decay():
        fl = _dot(u, wf_ref[...]) + bf_ref[...]
        log_f = jnp.minimum(fl, 0.0) - jnp.log(1.0 + jnp.exp(-jnp.abs(fl)))
        row = lax.broadcasted_iota(jnp.int32, (tm, tm), 0)
        col = lax.broadcasted_iota(jnp.int32, (tm, tm), 1)
        tri = jnp.where(row >= col, 1.0, 0.0).astype(BF16)
        h, m, l = _split3(log_f)
        cum = carry_ref[...] + ((_dot(tri, h) + _dot(tri, m)) + _dot(tri, l))
        carry_ref[...] = cum[tm - 1:tm, :]
        lane = lax.broadcasted_iota(jnp.int32, (tm, LANES), 1)
        ch, cm, cl = (t.astype(F32) for t in _split3(cum * LOG2E))
        pieces = jnp.where(lane < PIECE_LANE[1], ch,
                           jnp.where(lane < PIECE_LANE[2], cm,
                                     jnp.where(lane < ONE_LANE, cl,
                                               jnp.where(lane == ONE_LANE, 1.0, 0.0))))
        pieces = pieces.astype(BF16)
        augk_ref[0] = _dot(pieces, pselk_ref[...]).astype(BF16)
        augq_t_ref[0] = lax.dot_general(pselq_t_ref[...], pieces, NT_DIMS,
                                        preferred_element_type=F32).astype(BF16)

    half = ATTN_WIDTH // 2
    matmul_items = [lambda: emit_q(0, half), lambda: emit_q(half, ATTN_WIDTH),
                    lambda: emit_v(0, half), lambda: emit_v(half, ATTN_WIDTH),
                    emit_k, emit_decay]

    first = HALO - (CONV_KERNEL - 1)
    rows = tm + 2 * SUBLANES
    y = jnp.zeros((tm, CONV_WIDTH), F32) + bdw_ref[...]
    for b in range(SUBLANES):
        zb = None
        for a in range(pl.cdiv(CONV_KERNEL - b, SUBLANES)):
            k = SUBLANES * a + b
            term = wdw_ref[k:k + 1, :] * ext_ref[SUBLANES * a:SUBLANES * a + rows, :]
            zb = term if zb is None else zb + term
        y = y + zb[first + b:first + b + tm, :]
        if b < len(matmul_items):
            matmul_items[b]()
    ext_ref[0:HALO, :] = ext_ref[tm:tm + HALO, :]
    for r in range(0, tm, ROW_GROUP):
        cn_ref[0, r:r + ROW_GROUP, :] = _norm_conv_branch(y[r:r + ROW_GROUP], gng_ref, gnb_ref,
                                                          gout_ref, gmat_ref)


def _norm_conv_branch(y, gng_ref, gnb_ref, gout_ref, gmat_ref):
    gmat = gmat_ref[...]
    yh, yl = _split2(y)
    mu = _dot(yh, gmat) + _dot(yl, gmat)
    d = y - mu
    dh, dl = _split2(d * d)
    var = _dot(dh, gmat) + _dot(dl, gmat)
    yn = d * lax.rsqrt(var + LN_EPS) * gng_ref[...] + gnb_ref[...]
    sw = yn * jax.nn.sigmoid(yn)
    ms = jnp.mean(sw * sw, axis=-1, keepdims=True)
    return (sw * lax.rsqrt(ms + LN_EPS) * gout_ref[...]).astype(BF16)


def _decay_routing_matrices():
    selq_t = np.zeros((N_HEADS * AUG_ROWS, LANES), np.float32)
    selk = np.zeros((LANES, N_PAIRS * LANES), np.float32)
    for head in range(N_HEADS):
        p, j = divmod(head, 2)
        qbase = head * AUG_ROWS + AUG_PER_HEAD * j
        kbase = p * LANES + AUG_PER_HEAD * j
        for i in range(3):
            selq_t[qbase + i, PIECE_LANE[i] + head] = 1.0
            selq_t[qbase + 3 + i, ONE_LANE] = 1.0
            selk[ONE_LANE, kbase + i] = 1.0
            selk[PIECE_LANE[i] + head, kbase + 3 + i] = -1.0
    return jnp.asarray(selq_t, BF16), jnp.asarray(selk, BF16)


def _inproj(x, mod1, wqv_t, wk, wf, bf, wa, wg, pselq_t, pselk, *conv_params):
    b, s, d = x.shape
    tm = TM_PROJ
    const = lambda shape: pl.BlockSpec(shape, lambda bi, i: (0,) * len(shape))
    conv_specs = [const(p.shape) for p in conv_params]
    return pl.pallas_call(
        _inproj_kernel,
        out_shape=(jax.ShapeDtypeStruct((b, ATTN_WIDTH, s), BF16),
                   jax.ShapeDtypeStruct((b, s // TK, ATTN_WIDTH, TK), BF16),
                   jax.ShapeDtypeStruct((b, s, ATTN_WIDTH), BF16),
                   jax.ShapeDtypeStruct((b, N_HEADS * AUG_ROWS, s), BF16),
                   jax.ShapeDtypeStruct((b, s, N_PAIRS * LANES), BF16),
                   jax.ShapeDtypeStruct((b, s, CONV_WIDTH), BF16)),
        grid=(b, s // tm),
        in_specs=[pl.BlockSpec((1, tm, d), lambda bi, i: (bi, i, 0)),
                  pl.BlockSpec((1, 3, d), lambda bi, i: (bi, 0, 0)),
                  const(wqv_t.shape), const(wk.shape), const(wf.shape), const(bf.shape),
                  const(wa.shape), const(wg.shape), const(pselq_t.shape), const(pselk.shape)]
                 + conv_specs,
        out_specs=(pl.BlockSpec((1, ATTN_WIDTH, tm), lambda bi, i: (bi, 0, i)),
                   pl.BlockSpec((1, tm // TK, ATTN_WIDTH, TK), lambda bi, i: (bi, i, 0, 0)),
                   pl.BlockSpec((1, tm, ATTN_WIDTH), lambda bi, i: (bi, i, 0)),
                   pl.BlockSpec((1, N_HEADS * AUG_ROWS, tm), lambda bi, i: (bi, 0, i)),
                   pl.BlockSpec((1, tm, N_PAIRS * LANES), lambda bi, i: (bi, i, 0)),
                   pl.BlockSpec((1, tm, CONV_WIDTH), lambda bi, i: (bi, i, 0))),
        scratch_shapes=[pltpu.VMEM((1, LANES), F32),
                        pltpu.VMEM((HALO + tm + EXT_TAIL, CONV_WIDTH), F32)],
        compiler_params=pltpu.CompilerParams(
            dimension_semantics=("arbitrary", "arbitrary"), vmem_limit_bytes=VMEM_LIMIT),
        name="inproj",
    )(x, mod1, wqv_t, wk, wf, bf, wa, wg, pselq_t, pselk, *conv_params)


def _attn_kernel(qt_ref, aqt_ref, k_ref, ak_ref, vt_ref, o_ref, s_scr, acc_scr, st_scr):
    for sub in range(TILES_PER_STEP):
        cols = slice(sub * TQ, (sub + 1) * TQ)
        _attn_tile(pl.program_id(2) * TILES_PER_STEP + sub,
                   qt_ref.at[0, :, cols], aqt_ref.at[0, :, cols], k_ref, ak_ref, vt_ref,
                   o_ref.at[0, cols, :], s_scr.at[sub], acc_scr.at[sub], st_scr.at[sub])


def _attn_tile(qi, qt_ref, aqt_ref, k_ref, ak_ref, vt_ref, o_ref, s_scr, acc_scr, st_scr):
    tq = qt_ref.shape[1]
    assert tq == TK, "the drain handles exactly one diagonal key block"
    heads = range(HEADS_PER_STEP)

    zeros_half = jnp.zeros((HEAD_DIM, tq), BF16)
    zeros_tail = jnp.zeros((LANES - AUG_ROWS, tq), BF16)
    qcat_t = []
    for h in heads:
        qh = qt_ref[HEAD_DIM * h:HEAD_DIM * (h + 1), :]
        halves = [qh, zeros_half] if h % 2 == 0 else [zeros_half, qh]
        qcat_t.append(jnp.concatenate(
            halves + [aqt_ref[AUG_ROWS * h:AUG_ROWS * (h + 1), :], zeros_tail], axis=0))

    def logits(kb, h):
        start = pl.multiple_of(kb * TK, TK)
        pair = pl.ds((h // 2) * LANES, LANES)
        kcat = jnp.concatenate([k_ref[0, pl.ds(start, TK), pair], ak_ref[0, pl.ds(start, TK), pair]],
                               axis=-1)
        return _dot(kcat, qcat_t[h])

    def block_max(s):
        return jnp.max(s, axis=0, keepdims=True)

    def softmax_update(s, m_blk, m):
        m_new = jnp.maximum(m, m_blk)
        alpha = jnp.exp2(m - m_new)
        p = jnp.exp2((s - m_new).astype(BF16))
        return p, alpha, m_new

    ones_rows = jnp.ones((AUG_ROWS, TK), BF16)

    def accumulate(kb, h, p, alpha, acc):
        vt = jnp.concatenate([vt_ref[0, kb, HEAD_DIM * h:HEAD_DIM * (h + 1), :], ones_rows], axis=0)
        return alpha * acc + _dot(vt, p)

    def stat(h, j):
        return st_scr.at[pl.ds(2 * h + j, 1), :]

    for h in heads:
        s0 = logits(0, h)
        s_scr[h] = s0
        acc_scr[h] = jnp.zeros((HEAD_DIM + AUG_ROWS, tq), F32)
        stat(h, 0)[...] = jnp.full((1, tq), NEG_BIG, F32)
        stat(h, 1)[...] = block_max(s0)

    def step(kb):
        for h in heads:
            s_next = logits(kb + 1, h)
            p, alpha, m = softmax_update(s_scr[h], stat(h, 1)[...], stat(h, 0)[...])
            acc_scr[h] = accumulate(kb, h, p, alpha, acc_scr[h])
            s_scr[h] = s_next
            stat(h, 0)[...] = m
            stat(h, 1)[...] = block_max(s_next)

    done = 0
    for unroll in KV_UNROLLS:
        trips = (qi - done) // unroll

        def body(i, carry, unroll=unroll, base=done):
            for u in range(unroll):
                step(base + unroll * i + u)
            return carry

        lax.fori_loop(0, trips, body, 0)
        done = done + trips * unroll

    key = lax.broadcasted_iota(jnp.int32, (TK, tq), 0)
    qry = lax.broadcasted_iota(jnp.int32, (TK, tq), 1)
    outs = []
    for h in heads:
        s = jnp.where(key <= qry, s_scr[h], NEG_BIG)
        p, alpha, _ = softmax_update(s, block_max(s), stat(h, 0)[...])
        acc = accumulate(qi, h, p, alpha, acc_scr[h])
        outs.append(acc[:HEAD_DIM] / acc[HEAD_DIM:HEAD_DIM + 1])
    for pr in range(HEADS_PER_STEP // 2):
        o_ref[:, pr * LANES:(pr + 1) * LANES] = jnp.concatenate(outs[2 * pr:2 * pr + 2], axis=0).T


def _attn(qt, augq_t, k, augk, vt):
    b, s, _ = k.shape
    g, n = HEADS_PER_STEP, TILES_PER_STEP
    tq_step = n * TQ
    return pl.pallas_call(
        _attn_kernel,
        out_shape=jax.ShapeDtypeStruct((b, s, ATTN_WIDTH), F32),
        grid=(b, N_HEADS // g, s // tq_step),
        in_specs=[pl.BlockSpec((1, g * HEAD_DIM, tq_step), lambda bi, p, i: (bi, p, i)),
                  pl.BlockSpec((1, g * AUG_ROWS, tq_step), lambda bi, p, i: (bi, p, i)),
                  pl.BlockSpec((1, s, g * HEAD_DIM), lambda bi, p, i: (bi, 0, p),
                               pipeline_mode=pl.Buffered(1)),
                  pl.BlockSpec((1, s, g * HEAD_DIM), lambda bi, p, i: (bi, 0, p),
                               pipeline_mode=pl.Buffered(1)),
                  pl.BlockSpec((1, s // TK, g * HEAD_DIM, TK), lambda bi, p, i: (bi, 0, p, 0),
                               pipeline_mode=pl.Buffered(1))],
        out_specs=pl.BlockSpec((1, tq_step, g * HEAD_DIM), lambda bi, p, i: (bi, i, p)),
        scratch_shapes=[pltpu.VMEM((n, g, TK, TQ), F32),
                        pltpu.VMEM((n, g, HEAD_DIM + AUG_ROWS, TQ), F32),
                        pltpu.VMEM((n, 2 * g, TQ), F32)],
        compiler_params=pltpu.CompilerParams(
            dimension_semantics=("arbitrary", "arbitrary", "arbitrary"),
            vmem_limit_bytes=VMEM_LIMIT),
        name="attn",
    )(qt, augq_t, k, augk, vt)


def _layernorm(y, g, b):
    mu = jnp.mean(y, axis=-1, keepdims=True)
    d = y - mu
    var = jnp.mean(d * d, axis=-1, keepdims=True)
    return d * lax.rsqrt(var + LN_EPS) * g + b


def _mix_mlp_kernel(attn_ref, cn_ref, x_ref, mod_ref, gattn_ref, wo_a_ref, wo_c_ref,
                    ln1g_ref, ln1b_ref, w1_ref, w2_ref, ln2g_ref, ln2b_ref, o_ref):
    tm = x_ref.shape[1]
    groups = [slice(r, r + ROW_GROUP) for r in range(0, tm, ROW_GROUP)]
    gate1, shift2, scale2, gate2 = (mod_ref[0, j:j + 1, :] for j in range(2, 6))

    def token_mix(rows):
        a = attn_ref[0, rows, :]
        ms = jnp.mean(a * a, axis=-1, keepdims=True)
        an = (a * lax.rsqrt(ms + LN_EPS) * gattn_ref[...]).astype(BF16)
        mixed = _dot(an, wo_a_ref[...]) + _dot(cn_ref[0, rows, :], wo_c_ref[...])
        return DEEPNORM_ALPHA * x_ref[0, rows, :] + (1.0 + gate1) * mixed

    def mlp(x1):
        u = (x1 * (1.0 + scale2) + shift2).astype(BF16)
        ff = jnp.zeros(x1.shape, F32)
        for c in range(0, D_FF, FF_CHUNK):
            hid = jnp.maximum(_dot(u, w1_ref[:, c:c + FF_CHUNK]), 0.0)
            ff = ff + _dot((hid * hid).astype(BF16), w2_ref[c:c + FF_CHUNK, :])
        return DEEPNORM_ALPHA * x1 + (1.0 + gate2) * ff

    x1 = [_layernorm(token_mix(rows), ln1g_ref[...], ln1b_ref[...]) for rows in groups]
    for rows, x1_rows in zip(groups, x1):
        o_ref[0, rows, :] = _layernorm(mlp(x1_rows), ln2g_ref[...], ln2b_ref[...])


def _mix_mlp(attn, cn, x, mods, gattn, wo_a, wo_c, ln1g, ln1b, w1, w2, ln2g, ln2b):
    b, s, d = x.shape
    tm = TM_PROJ
    const = lambda a: pl.BlockSpec(a.shape, lambda bi, i: (0,) * a.ndim, pipeline_mode=pl.Buffered(1))
    return pl.pallas_call(
        _mix_mlp_kernel,
        out_shape=jax.ShapeDtypeStruct((b, s, d), F32),
        grid=(b, s // tm),
        in_specs=[pl.BlockSpec((1, tm, ATTN_WIDTH), lambda bi, i: (bi, i, 0)),
                  pl.BlockSpec((1, tm, CONV_WIDTH), lambda bi, i: (bi, i, 0)),
                  pl.BlockSpec((1, tm, d), lambda bi, i: (bi, i, 0)),
                  pl.BlockSpec((1, 6, d), lambda bi, i: (bi, 0, 0)),
                  const(gattn), const(wo_a), const(wo_c), const(ln1g), const(ln1b),
                  const(w1), const(w2), const(ln2g), const(ln2b)],
        out_specs=pl.BlockSpec((1, tm, d), lambda bi, i: (bi, i, 0)),
        compiler_params=pltpu.CompilerParams(
            dimension_semantics=("arbitrary", "arbitrary"), vmem_limit_bytes=VMEM_LIMIT),
        name="mix_mlp",
    )(attn, cn, x, mods, gattn, wo_a, wo_c, ln1g, ln1b, w1, w2, ln2g, ln2b)


def kernel(x, c, w_ada, b_ada, w_in, b_forget, w_dw, b_dw, gn_g, gn_b, g_attn_out, g_conv_out,
           w_out, ln1_g, ln1_b, w_ff1, w_ff2, ln2_g, ln2_b):
    bsz = x.shape[0]
    layer = 0
    row = lambda v: v.reshape(1, -1)

    c_pad = jnp.pad(c, ((0, 2 * SUBLANES - bsz), (0, 0)))
    ada = _ada(c_pad, w_ada[layer], row(b_ada[layer]))[:bsz]
    mods = ada.reshape(bsz, 6, D_MODEL)
    mod1 = mods[:, 0:3]

    w = w_in[layer]
    a0 = 3 * ATTN_WIDTH + N_HEADS
    wq, wk, wv = (w[:, i * ATTN_WIDTH:(i + 1) * ATTN_WIDTH] for i in range(3))
    wqv_t = jnp.concatenate([wq, wv], axis=1).T.astype(BF16)
    wk = wk.astype(BF16)
    reps = LANES // N_HEADS
    wf = jnp.tile(w[:, 3 * ATTN_WIDTH:a0], (1, reps)).astype(BF16)
    bf = jnp.tile(b_forget[layer], reps).reshape(1, LANES)
    wa = w[:, a0:a0 + CONV_WIDTH].astype(BF16)
    wg = w[:, a0 + CONV_WIDTH:].astype(BF16)

    pselq_t, pselk = _decay_routing_matrices()
    grp = np.arange(CONV_WIDTH) // CONV_GROUP
    gmat = jnp.asarray((grp[:, None] == grp[None, :]).astype(np.float32) / CONV_GROUP, BF16)
    qt, vt, k, augq_t, augk, cn = _inproj(
        x, mod1, wqv_t, wk, wf, bf, wa, wg, pselq_t, pselk,
        w_dw[layer].reshape(CONV_KERNEL, CONV_WIDTH), row(b_dw[layer]),
        row(gn_g[layer]), row(gn_b[layer]), row(g_conv_out[layer]), gmat)

    attn = _attn(qt, augq_t, k, augk, vt)

    wo = w_out[layer].astype(BF16)
    return _mix_mlp(attn, cn, x, mods, row(g_attn_out[layer]), wo[:ATTN_WIDTH], wo[ATTN_WIDTH:],
                    row(ln1_g[layer]), row(ln1_b[layer]),
                    w_ff1[layer].astype(BF16), w_ff2[layer].astype(BF16),
                    row(ln2_g[layer]), row(ln2_b[layer]))
```

```python
import functools

import numpy as np
import jax
import jax.numpy as jnp
from jax import lax
from jax.experimental import pallas as pl
from jax.experimental.pallas import tpu as pltpu

D_MODEL = 1024
HEAD_DIM = 64
ATTN_WIDTH = 512
CONV_WIDTH = 512
N_HEADS = 8
N_PAIRS = N_HEADS // 2
CONV_KERNEL = 31
CONV_GROUP = 64
D_FF = 4 * D_MODEL
LN_EPS = 1e-5
DEEPNORM_ALPHA = 2.0 ** 0.25

LANES = 128
SUBLANES = 8
HALO = 32
EXT_TAIL = 8
AUG_PER_HEAD = 6
AUG_ROWS = 16
NT_DIMS = (((1,), (1,)), ((), ()))
PIECE_LANE = (0, 8, 16)
ONE_LANE = 24
NEG_BIG = -1e30

TM_PROJ = 512
ROW_GROUP = 256
FF_CHUNK = 1024
WEIGHT_CHUNKS = 8
TQ = 256
TK = 256
HEADS_PER_STEP = 8
KV_UNROLLS = (4, 2, 1)
TILES_PER_STEP = 2
LOG2E = 1.4426950408889634
VMEM_LIMIT = 56 * 1024 * 1024

F32 = jnp.float32
BF16 = jnp.bfloat16


def _split3(x):
    hi = x.astype(BF16)
    r = x - hi.astype(F32)
    mid = r.astype(BF16)
    lo = (r - mid.astype(F32)).astype(BF16)
    return hi, mid, lo


def _split2(x):
    hi = x.astype(BF16)
    lo = (x - hi.astype(F32)).astype(BF16)
    return hi, lo


def _dot(a, b):
    return jnp.dot(a, b, preferred_element_type=F32)


def _ada_kernel(c_ref, w_ref, b_ref, o_ref):
    c = c_ref[...]
    sh, sl = _split2(c * jax.nn.sigmoid(c))
    wh, wl = _split2(w_ref[...])
    o_ref[...] = (_dot(sh, wh) + (_dot(sl, wh) + _dot(sh, wl))) + b_ref[...]


def _ada(c_pad, w_ada, b_ada):
    n = w_ada.shape[1]
    tn = 1536
    return pl.pallas_call(
        _ada_kernel,
        out_shape=jax.ShapeDtypeStruct((c_pad.shape[0], n), F32),
        grid=(n // tn,),
        in_specs=[pl.BlockSpec(c_pad.shape, lambda j: (0, 0)),
                  pl.BlockSpec((D_MODEL, tn), lambda j: (0, j)),
                  pl.BlockSpec((1, tn), lambda j: (0, j))],
        out_specs=pl.BlockSpec((c_pad.shape[0], tn), lambda j: (0, j)),
        name="ada",
    )(c_pad, w_ada, b_ada)


def _inproj_kernel(x_ref, mod_ref, wqv_t_ref, wk_ref, wf_ref, bf_ref, wa_ref, wg_ref,
                   pselq_t_ref, pselk_ref, wdw_ref, bdw_ref, gng_ref, gnb_ref, gout_ref, gmat_ref,
                   qt_ref, vt_ref, k_ref, augq_t_ref, augk_ref, cn_ref, carry_ref, ext_ref):
    tm = x_ref.shape[1]

    @pl.when(pl.program_id(1) == 0)
    def _():
        carry_ref[...] = jnp.zeros_like(carry_ref)
        ext_ref[0:HALO, :] = jnp.zeros((HALO, CONV_WIDTH), F32)
        ext_ref[HALO + tm:, :] = jnp.zeros((EXT_TAIL, CONV_WIDTH), F32)

    x = x_ref[0]
    shift = mod_ref[0, 0:1, :]
    scale = mod_ref[0, 1:2, :]
    u = (x * (1.0 + scale) + shift).astype(BF16)

    ext_ref[HALO:HALO + tm, :] = _dot(u, wa_ref[...]) * jax.nn.sigmoid(_dot(u, wg_ref[...]))

    def qv_rows(lo, hi):
        return lax.dot_general(wqv_t_ref[lo:hi, :], u, NT_DIMS, preferred_element_type=F32)

    def emit_q(lo, hi):
        qt_ref[0, lo:hi, :] = (qv_rows(lo, hi) * (LOG2E * HEAD_DIM ** -0.5)).astype(BF16)

    def emit_v(lo, hi):
        v_t = qv_rows(ATTN_WIDTH + lo, ATTN_WIDTH + hi).astype(BF16)
        for t in range(tm // TK):
            vt_ref[0, t, lo:hi, :] = v_t[:, t * TK:(t + 1) * TK]

    def emit_k():
        k_ref[0] = _dot(u, wk_ref[...]).astype(BF16)

    def emit_decay():
        fl = _dot(u, wf_ref[...]) + bf_ref[...]
        log_f = jnp.minimum(fl, 0.0) - jnp.log(1.0 + jnp.exp(-jnp.abs(fl)))
        row = lax.broadcasted_iota(jnp.int32, (tm, tm), 0)
        col = lax.broadcasted_iota(jnp.int32, (tm, tm), 1)
        tri = jnp.where(row >= col, 1.0, 0.0).astype(BF16)
        h, m, l = _split3(log_f)
        cum = carry_ref[...] + ((_dot(tri, h) + _dot(tri, m)) + _dot(tri, l))
        carry_ref[...] = cum[tm - 1:tm, :]
        lane = lax.broadcasted_iota(jnp.int32, (tm, LANES), 1)
        ch, cm, cl = (t.astype(F32) for t in _split3(cum * LOG2E))
        pieces = jnp.where(lane < PIECE_LANE[1], ch,
                           jnp.where(lane < PIECE_LANE[2], cm,
                                     jnp.where(lane < ONE_LANE, cl,
                                               jnp.where(lane == ONE_LANE, 1.0, 0.0))))
        pieces = pieces.astype(BF16)
        augk_ref[0] = _dot(pieces, pselk_ref[...]).astype(BF16)
        augq_t_ref[0] = lax.dot_general(pselq_t_ref[...], pieces, NT_DIMS,
                                        preferred_element_type=F32).astype(BF16)

    half = ATTN_WIDTH // 2
    matmul_items = [lambda: emit_q(0, half), lambda: emit_q(half, ATTN_WIDTH),
                    lambda: emit_v(0, half), lambda: emit_v(half, ATTN_WIDTH),
                    emit_k, emit_decay]

    first = HALO - (CONV_KERNEL - 1)
    rows = tm + 2 * SUBLANES
    y = jnp.zeros((tm, CONV_WIDTH), F32) + bdw_ref[...]
    for b in range(SUBLANES):
        zb = None
        for a in range(pl.cdiv(CONV_KERNEL - b, SUBLANES)):
            k = SUBLANES * a + b
            term = wdw_ref[k:k + 1, :] * ext_ref[SUBLANES * a:SUBLANES * a + rows, :]
            zb = term if zb is None else zb + term
        y = y + zb[first + b:first + b + tm, :]
        if b < len(matmul_items):
            matmul_items[b]()
    ext_ref[0:HALO, :] = ext_ref[tm:tm + HALO, :]
    for r in range(0, tm, ROW_GROUP):
        cn_ref[0, r:r + ROW_GROUP, :] = _norm_conv_branch(y[r:r + ROW_GROUP], gng_ref, gnb_ref,
                                                          gout_ref, gmat_ref)


def _norm_conv_branch(y, gng_ref, gnb_ref, gout_ref, gmat_ref):
    gmat = gmat_ref[...]
    yh, yl = _split2(y)
    mu = _dot(yh, gmat) + _dot(yl, gmat)
    d = y - mu
    dh, dl = _split2(d * d)
    var = _dot(dh, gmat) + _dot(dl, gmat)
    yn = d * lax.rsqrt(var + LN_EPS) * gng_ref[...] + gnb_ref[...]
    sw = yn * jax.nn.sigmoid(yn)
    ms = jnp.mean(sw * sw, axis=-1, keepdims=True)
    return (sw * lax.rsqrt(ms + LN_EPS) * gout_ref[...]).astype(BF16)


def _decay_routing_matrices():
    selq_t = np.zeros((N_HEADS * AUG_ROWS, LANES), np.float32)
    selk = np.zeros((LANES, N_PAIRS * LANES), np.float32)
    for head in range(N_HEADS):
        p, j = divmod(head, 2)
        qbase = head * AUG_ROWS + AUG_PER_HEAD * j
        kbase = p * LANES + AUG_PER_HEAD * j
        for i in range(3):
            selq_t[qbase + i, PIECE_LANE[i] + head] = 1.0
            selq_t[qbase + 3 + i, ONE_LANE] = 1.0
            selk[ONE_LANE, kbase + i] = 1.0
            selk[PIECE_LANE[i] + head, kbase + 3 + i] = -1.0
    return jnp.asarray(selq_t, BF16), jnp.asarray(selk, BF16)


def _inproj(x, mod1, wqv_t, wk, wf, bf, wa, wg, pselq_t, pselk, *conv_params):
    b, s, d = x.shape
    tm = TM_PROJ
    const = lambda shape: pl.BlockSpec(shape, lambda bi, i: (0,) * len(shape))
    conv_specs = [const(p.shape) for p in conv_params]
    return pl.pallas_call(
        _inproj_kernel,
        out_shape=(jax.ShapeDtypeStruct((b, ATTN_WIDTH, s), BF16),
                   jax.ShapeDtypeStruct((b, s // TK, ATTN_WIDTH, TK), BF16),
                   jax.ShapeDtypeStruct((b, s, ATTN_WIDTH), BF16),
                   jax.ShapeDtypeStruct((b, N_HEADS * AUG_ROWS, s), BF16),
                   jax.ShapeDtypeStruct((b, s, N_PAIRS * LANES), BF16),
                   jax.ShapeDtypeStruct((b, s, CONV_WIDTH), BF16)),
        grid=(b, s // tm),
        in_specs=[pl.BlockSpec((1, tm, d), lambda bi, i: (bi, i, 0)),
                  pl.BlockSpec((1, 3, d), lambda bi, i: (bi, 0, 0)),
                  const(wqv_t.shape), const(wk.shape), const(wf.shape), const(bf.shape),
                  const(wa.shape), const(wg.shape), const(pselq_t.shape), const(pselk.shape)]
                 + conv_specs,
        out_specs=(pl.BlockSpec((1, ATTN_WIDTH, tm), lambda bi, i: (bi, 0, i)),
                   pl.BlockSpec((1, tm // TK, ATTN_WIDTH, TK), lambda bi, i: (bi, i, 0, 0)),
                   pl.BlockSpec((1, tm, ATTN_WIDTH), lambda bi, i: (bi, i, 0)),
                   pl.BlockSpec((1, N_HEADS * AUG_ROWS, tm), lambda bi, i: (bi, 0, i)),
                   pl.BlockSpec((1, tm, N_PAIRS * LANES), lambda bi, i: (bi, i, 0)),
                   pl.BlockSpec((1, tm, CONV_WIDTH), lambda bi, i: (bi, i, 0))),
        scratch_shapes=[pltpu.VMEM((1, LANES), F32),
                        pltpu.VMEM((HALO + tm + EXT_TAIL, CONV_WIDTH), F32)],
        compiler_params=pltpu.CompilerParams(
            dimension_semantics=("arbitrary", "arbitrary"), vmem_limit_bytes=VMEM_LIMIT),
        name="inproj",
    )(x, mod1, wqv_t, wk, wf, bf, wa, wg, pselq_t, pselk, *conv_params)


def _attn_kernel(qt_ref, aqt_ref, k_ref, ak_ref, vt_ref, o_ref, s_scr, acc_scr, st_scr):
    for sub in range(TILES_PER_STEP):
        cols = slice(sub * TQ, (sub + 1) * TQ)
        _attn_tile(pl.program_id(2) * TILES_PER_STEP + sub,
                   qt_ref.at[0, :, cols], aqt_ref.at[0, :, cols], k_ref, ak_ref, vt_ref,
                   o_ref.at[0, cols, :], s_scr.at[sub], acc_scr.at[sub], st_scr.at[sub])


def _attn_tile(qi, qt_ref, aqt_ref, k_ref, ak_ref, vt_ref, o_ref, s_scr, acc_scr, st_scr):
    tq = qt_ref.shape[1]
    assert tq == TK, "the drain handles exactly one diagonal key block"
    heads = range(HEADS_PER_STEP)

    zeros_half = jnp.zeros((HEAD_DIM, tq), BF16)
    zeros_tail = jnp.zeros((LANES - AUG_ROWS, tq), BF16)
    qcat_t = []
    for h in heads:
        qh = qt_ref[HEAD_DIM * h:HEAD_DIM * (h + 1), :]
        halves = [qh, zeros_half] if h % 2 == 0 else [zeros_half, qh]
        qcat_t.append(jnp.concatenate(
            halves + [aqt_ref[AUG_ROWS * h:AUG_ROWS * (h + 1), :], zeros_tail], axis=0))

    def logits(kb, h):
        start = pl.multiple_of(kb * TK, TK)
        pair = pl.ds((h // 2) * LANES, LANES)
        kcat = jnp.concatenate([k_ref[0, pl.ds(start, TK), pair], ak_ref[0, pl.ds(start, TK), pair]],
                               axis=-1)
        return _dot(kcat, qcat_t[h])

    def block_max(s):
        return jnp.max(s, axis=0, keepdims=True)

    def softmax_update(s, m_blk, m):
        m_new = jnp.maximum(m, m_blk)
        alpha = jnp.exp2(m - m_new)
        p = jnp.exp2((s - m_new).astype(BF16))
        return p, alpha, m_new

    ones_rows = jnp.ones((AUG_ROWS, TK), BF16)

    def accumulate(kb, h, p, alpha, acc):
        vt = jnp.concatenate([vt_ref[0, kb, HEAD_DIM * h:HEAD_DIM * (h + 1), :], ones_rows], axis=0)
        return alpha * acc + _dot(vt, p)

    def stat(h, j):
        return st_scr.at[pl.ds(2 * h + j, 1), :]

    for h in heads:
        s0 = logits(0, h)
        s_scr[h] = s0
        acc_scr[h] = jnp.zeros((HEAD_DIM + AUG_ROWS, tq), F32)
        stat(h, 0)[...] = jnp.full((1, tq), NEG_BIG, F32)
        stat(h, 1)[...] = block_max(s0)

    def step(kb):
        for h in heads:
            s_next = logits(kb + 1, h)
            p, alpha, m = softmax_update(s_scr[h], stat(h, 1)[...], stat(h, 0)[...])
            acc_scr[h] = accumulate(kb, h, p, alpha, acc_scr[h])
            s_scr[h] = s_next
            stat(h, 0)[...] = m
            stat(h, 1)[...] = block_max(s_next)

    done = 0
    for unroll in KV_UNROLLS:
        trips = (qi - done) // unroll

        def body(i, carry, unroll=unroll, base=done):
            for u in range(unroll):
                step(base + unroll * i + u)
            return carry

        lax.fori_loop(0, trips, body, 0)
        done = done + trips * unroll

    key = lax.broadcasted_iota(jnp.int32, (TK, tq), 0)
    qry = lax.broadcasted_iota(jnp.int32, (TK, tq), 1)
    outs = []
    for h in heads:
        s = jnp.where(key <= qry, s_scr[h], NEG_BIG)
        p, alpha, _ = softmax_update(s, block_max(s), stat(h, 0)[...])
        acc = accumulate(qi, h, p, alpha, acc_scr[h])
        outs.append(acc[:HEAD_DIM] / acc[HEAD_DIM:HEAD_DIM + 1])
    for pr in range(HEADS_PER_STEP // 2):
        o_ref[:, pr * LANES:(pr + 1) * LANES] = jnp.concatenate(outs[2 * pr:2 * pr + 2], axis=0).T


def _attn(qt, augq_t, k, augk, vt):
    b, s, _ = k.shape
    g, n = HEADS_PER_STEP, TILES_PER_STEP
    tq_step = n * TQ
    return pl.pallas_call(
        _attn_kernel,
        out_shape=jax.ShapeDtypeStruct((b, s, ATTN_WIDTH), F32),
        grid=(b, N_HEADS // g, s // tq_step),
        in_specs=[pl.BlockSpec((1, g * HEAD_DIM, tq_step), lambda bi, p, i: (bi, p, i)),
                  pl.BlockSpec((1, g * AUG_ROWS, tq_step), lambda bi, p, i: (bi, p, i)),
                  pl.BlockSpec((1, s, g * HEAD_DIM), lambda bi, p, i: (bi, 0, p),
                               pipeline_mode=pl.Buffered(1)),
                  pl.BlockSpec((1, s, g * HEAD_DIM), lambda bi, p, i: (bi, 0, p),
                               pipeline_mode=pl.Buffered(1)),
                  pl.BlockSpec((1, s // TK, g * HEAD_DIM, TK), lambda bi, p, i: (bi, 0, p, 0),
                               pipeline_mode=pl.Buffered(1))],
        out_specs=pl.BlockSpec((1, tq_step, g * HEAD_DIM), lambda bi, p, i: (bi, i, p)),
        scratch_shapes=[pltpu.VMEM((n, g, TK, TQ), F32),
                        pltpu.VMEM((n, g, HEAD_DIM + AUG_ROWS, TQ), F32),
                        pltpu.VMEM((n, 2 * g, TQ), F32)],
        compiler_params=pltpu.CompilerParams(
            dimension_semantics=("arbitrary", "arbitrary", "arbitrary"),
            vmem_limit_bytes=VMEM_LIMIT),
        name="attn",
    )(qt, augq_t, k, augk, vt)


def _layernorm(y, g, b):
    mu = jnp.mean(y, axis=-1, keepdims=True)
    d = y - mu
    var = jnp.mean(d * d, axis=-1, keepdims=True)
    return d * lax.rsqrt(var + LN_EPS) * g + b


def _mix_mlp_kernel(attn_ref, cn_ref, x_ref, mod_ref, gattn_ref, wo_a_ref, wo_c_ref,
                    ln1g_ref, ln1b_ref, w1_hbm, w2_hbm, ln2g_ref, ln2b_ref, o_ref,
                    w1_ref, w2_ref, stage1, stage2, sem):
    tm = x_ref.shape[1]

    @pl.when((pl.program_id(0) == 0) & (pl.program_id(1) == 0))
    def _():
        def stage_in(src, dst, stage):
            rows = stage.shape[1]
            n = src.shape[0] // rows

            def copy(c):
                return pltpu.make_async_copy(src.at[pl.ds(c * rows, rows), :], stage.at[c % 2],
                                             sem.at[c % 2])

            copy(0).start()
            for c in range(n):
                if c + 1 < n:
                    copy(c + 1).start()
                copy(c).wait()
                dst[c * rows:(c + 1) * rows, :] = stage[c % 2].astype(BF16)

        stage_in(w1_hbm, w1_ref, stage1)
        stage_in(w2_hbm, w2_ref, stage2)

    groups = [slice(r, r + ROW_GROUP) for r in range(0, tm, ROW_GROUP)]
    gate1, shift2, scale2, gate2 = (mod_ref[0, j:j + 1, :] for j in range(2, 6))

    def token_mix(rows):
        a = attn_ref[0, rows, :]
        ms = jnp.mean(a * a, axis=-1, keepdims=True)
        an = (a * lax.rsqrt(ms + LN_EPS) * gattn_ref[...]).astype(BF16)
        mixed = _dot(an, wo_a_ref[...]) + _dot(cn_ref[0, rows, :], wo_c_ref[...])
        return DEEPNORM_ALPHA * x_ref[0, rows, :] + (1.0 + gate1) * mixed

    def mlp(x1):
        u = (x1 * (1.0 + scale2) + shift2).astype(BF16)
        ff = jnp.zeros(x1.shape, F32)
        for c in range(0, D_FF, FF_CHUNK):
            hid = jnp.maximum(_dot(u, w1_ref[:, c:c + FF_CHUNK]), 0.0)
            ff = ff + _dot((hid * hid).astype(BF16), w2_ref[c:c + FF_CHUNK, :])
        return DEEPNORM_ALPHA * x1 + (1.0 + gate2) * ff

    x1 = [_layernorm(token_mix(rows), ln1g_ref[...], ln1b_ref[...]) for rows in groups]
    for rows, x1_rows in zip(groups, x1):
        o_ref[0, rows, :] = _layernorm(mlp(x1_rows), ln2g_ref[...], ln2b_ref[...])


def _mix_mlp(attn, cn, x, mods, gattn, wo_a, wo_c, ln1g, ln1b, w1, w2, ln2g, ln2b):
    b, s, d = x.shape
    tm = TM_PROJ
    const = lambda a: pl.BlockSpec(a.shape, lambda bi, i: (0,) * a.ndim, pipeline_mode=pl.Buffered(1))
    return pl.pallas_call(
        _mix_mlp_kernel,
        out_shape=jax.ShapeDtypeStruct((b, s, d), F32),
        grid=(b, s // tm),
        in_specs=[pl.BlockSpec((1, tm, ATTN_WIDTH), lambda bi, i: (bi, i, 0)),
                  pl.BlockSpec((1, tm, CONV_WIDTH), lambda bi, i: (bi, i, 0)),
                  pl.BlockSpec((1, tm, d), lambda bi, i: (bi, i, 0)),
                  pl.BlockSpec((1, 6, d), lambda bi, i: (bi, 0, 0)),
                  const(gattn), const(wo_a), const(wo_c), const(ln1g), const(ln1b),
                  pl.BlockSpec(memory_space=pl.ANY), pl.BlockSpec(memory_space=pl.ANY),
                  const(ln2g), const(ln2b)],
        out_specs=pl.BlockSpec((1, tm, d), lambda bi, i: (bi, i, 0)),
        scratch_shapes=[pltpu.VMEM(w1.shape, BF16), pltpu.VMEM(w2.shape, BF16),
                        pltpu.VMEM((2, w1.shape[0] // WEIGHT_CHUNKS, w1.shape[1]), F32),
                        pltpu.VMEM((2, w2.shape[0] // WEIGHT_CHUNKS, w2.shape[1]), F32),
                        pltpu.SemaphoreType.DMA((2,))],
        compiler_params=pltpu.CompilerParams(
            dimension_semantics=("arbitrary", "arbitrary"), vmem_limit_bytes=VMEM_LIMIT),
        name="mix_mlp",
    )(attn, cn, x, mods, gattn, wo_a, wo_c, ln1g, ln1b, w1, w2, ln2g, ln2b)


def kernel(x, c, w_ada, b_ada, w_in, b_forget, w_dw, b_dw, gn_g, gn_b, g_attn_out, g_conv_out,
           w_out, ln1_g, ln1_b, w_ff1, w_ff2, ln2_g, ln2_b):
    bsz = x.shape[0]
    layer = 0
    row = lambda v: v.reshape(1, -1)

    c_pad = jnp.pad(c, ((0, 2 * SUBLANES - bsz), (0, 0)))
    ada = _ada(c_pad, w_ada[layer], row(b_ada[layer]))[:bsz]
    mods = ada.reshape(bsz, 6, D_MODEL)
    mod1 = mods[:, 0:3]

    w = w_in[layer]
    a0 = 3 * ATTN_WIDTH + N_HEADS
    wq, wk, wv = (w[:, i * ATTN_WIDTH:(i + 1) * ATTN_WIDTH] for i in range(3))
    wqv_t = jnp.concatenate([wq, wv], axis=1).T.astype(BF16)
    wk = wk.astype(BF16)
    reps = LANES // N_HEADS
    wf = jnp.tile(w[:, 3 * ATTN_WIDTH:a0], (1, reps)).astype(BF16)
    bf = jnp.tile(b_forget[layer], reps).reshape(1, LANES)
    wa = w[:, a0:a0 + CONV_WIDTH].astype(BF16)
    wg = w[:, a0 + CONV_WIDTH:].astype(BF16)

    pselq_t, pselk = _decay_routing_matrices()
    grp = np.arange(CONV_WIDTH) // CONV_GROUP
    gmat = jnp.asarray((grp[:, None] == grp[None, :]).astype(np.float32) / CONV_GROUP, BF16)
    qt, vt, k, augq_t, augk, cn = _inproj(
        x, mod1, wqv_t, wk, wf, bf, wa, wg, pselq_t, pselk,
        w_dw[layer].reshape(CONV_KERNEL, CONV_WIDTH), row(b_dw[layer]),
        row(gn_g[layer]), row(gn_b[layer]), row(g_conv_out[layer]), gmat)

    attn = _attn(qt, augq_t, k, augk, vt)

    wo = w_out[layer].astype(BF16)
    return _mix_mlp(attn, cn, x, mods, row(g_attn_out[layer]), wo[:ATTN_WIDTH], wo[ATTN_WIDTH:],
                    row(ln1_g[layer]), row(ln1_b[layer]),
                    w_ff1[layer], w_ff2[layer],
                    row(ln2_g[layer]), row(ln2_b[layer]))
```

```python
import functools

import numpy as np
import jax
import jax.numpy as jnp
from jax import lax
from jax.experimental import pallas as pl
from jax.experimental.pallas import tpu as pltpu

D_MODEL = 1024
HEAD_DIM = 64
ATTN_WIDTH = 512
CONV_WIDTH = 512
N_HEADS = 8
N_PAIRS = N_HEADS // 2
CONV_KERNEL = 31
CONV_GROUP = 64
D_FF = 4 * D_MODEL
LN_EPS = 1e-5
DEEPNORM_ALPHA = 2.0 ** 0.25

LANES = 128
SUBLANES = 8
HALO = 32
EXT_TAIL = 8
AUG_PER_HEAD = 6
AUG_ROWS = 16
NT_DIMS = (((1,), (1,)), ((), ()))
PIECE_LANE = (0, 8, 16)
ONE_LANE = 24
NEG_BIG = -1e30

TM_PROJ = 512
ROW_GROUP = 256
FF_CHUNK = 1024
TQ = 256
TK = 256
HEADS_PER_STEP = 8
KV_UNROLLS = (4, 2, 1)
TILES_PER_STEP = 2
LOG2E = 1.4426950408889634
VMEM_LIMIT = 56 * 1024 * 1024

F32 = jnp.float32
BF16 = jnp.bfloat16


def _split3(x):
    hi = x.astype(BF16)
    r = x - hi.astype(F32)
    mid = r.astype(BF16)
    lo = (r - mid.astype(F32)).astype(BF16)
    return hi, mid, lo


def _split2(x):
    hi = x.astype(BF16)
    lo = (x - hi.astype(F32)).astype(BF16)
    return hi, lo


def _dot(a, b):
    return jnp.dot(a, b, preferred_element_type=F32)


def _ada_kernel(c_ref, w_ref, b_ref, o_ref):
    c = c_ref[...]
    sh, sl = _split2(c * jax.nn.sigmoid(c))
    wh, wl = _split2(w_ref[...])
    o_ref[...] = (_dot(sh, wh) + (_dot(sl, wh) + _dot(sh, wl))) + b_ref[...]


def _ada(c_pad, w_ada, b_ada):
    n = w_ada.shape[1]
    tn = 1536
    return pl.pallas_call(
        _ada_kernel,
        out_shape=jax.ShapeDtypeStruct((c_pad.shape[0], n), F32),
        grid=(n // tn,),
        in_specs=[pl.BlockSpec(c_pad.shape, lambda j: (0, 0)),
                  pl.BlockSpec((D_MODEL, tn), lambda j: (0, j)),
                  pl.BlockSpec((1, tn), lambda j: (0, j))],
        out_specs=pl.BlockSpec((c_pad.shape[0], tn), lambda j: (0, j)),
        name="ada",
    )(c_pad, w_ada, b_ada)


def _inproj_kernel(x_ref, mod_ref, wqv_t_ref, wk_ref, wf_ref, bf_ref, wa_ref, wg_ref,
                   pselq_t_ref, pselk_ref, wdw_ref, bdw_ref, gng_ref, gnb_ref, gout_ref, gmat_ref,
                   qt_ref, vt_ref, k_ref, augq_t_ref, augk_ref, cn_ref, carry_ref, ext_ref):
    tm = x_ref.shape[1]

    @pl.when(pl.program_id(1) == 0)
    def _():
        carry_ref[...] = jnp.zeros_like(carry_ref)
        ext_ref[0:HALO, :] = jnp.zeros((HALO, CONV_WIDTH), F32)
        ext_ref[HALO + tm:, :] = jnp.zeros((EXT_TAIL, CONV_WIDTH), F32)

    x = x_ref[0]
    shift = mod_ref[0, 0:1, :]
    scale = mod_ref[0, 1:2, :]
    u = (x * (1.0 + scale) + shift).astype(BF16)

    ext_ref[HALO:HALO + tm, :] = _dot(u, wa_ref[...]) * jax.nn.sigmoid(_dot(u, wg_ref[...]))

    def qv_rows(lo, hi):
        return lax.dot_general(wqv_t_ref[lo:hi, :], u, NT_DIMS, preferred_element_type=F32)

    def emit_q(lo, hi):
        qt_ref[0, lo:hi, :] = (qv_rows(lo, hi) * (LOG2E * HEAD_DIM ** -0.5)).astype(BF16)

    def emit_v(lo, hi):
        v_t = qv_rows(ATTN_WIDTH + lo, ATTN_WIDTH + hi).astype(BF16)
        for t in range(tm // TK):
            vt_ref[0, t, lo:hi, :] = v_t[:, t * TK:(t + 1) * TK]

    def emit_k():
        k_ref[0] = _dot(u, wk_ref[...]).astype(BF16)

    def emit_decay():
        fl = _dot(u, wf_ref[...]) + bf_ref[...]
        log_f = jnp.minimum(fl, 0.0) - jnp.log(1.0 + jnp.exp(-jnp.abs(fl)))
        row = lax.broadcasted_iota(jnp.int32, (tm, tm), 0)
        col = lax.broadcasted_iota(jnp.int32, (tm, tm), 1)
        tri = jnp.where(row >= col, 1.0, 0.0).astype(BF16)
        h, m, l = _split3(log_f)
        cum = carry_ref[...] + ((_dot(tri, h) + _dot(tri, m)) + _dot(tri, l))
        carry_ref[...] = cum[tm - 1:tm, :]
        lane = lax.broadcasted_iota(jnp.int32, (tm, LANES), 1)
        ch, cm, cl = (t.astype(F32) for t in _split3(cum * LOG2E))
        pieces = jnp.where(lane < PIECE_LANE[1], ch,
                           jnp.where(lane < PIECE_LANE[2], cm,
                                     jnp.where(lane < ONE_LANE, cl,
                                               jnp.where(lane == ONE_LANE, 1.0, 0.0))))
        pieces = pieces.astype(BF16)
        augk_ref[0] = _dot(pieces, pselk_ref[...]).astype(BF16)
        augq_t_ref[0] = lax.dot_general(pselq_t_ref[...], pieces, NT_DIMS,
                                        preferred_element_type=F32).astype(BF16)

    half = ATTN_WIDTH // 2
    matmul_items = [lambda: emit_q(0, half), lambda: emit_q(half, ATTN_WIDTH),
                    lambda: emit_v(0, half), lambda: emit_v(half, ATTN_WIDTH),
                    emit_k, emit_decay]

    first = HALO - (CONV_KERNEL - 1)
    rows = tm + 2 * SUBLANES
    y = jnp.zeros((tm, CONV_WIDTH), F32) + bdw_ref[...]
    for b in range(SUBLANES):
        zb = None
        for a in range(pl.cdiv(CONV_KERNEL - b, SUBLANES)):
            k = SUBLANES * a + b
            term = wdw_ref[k:k + 1, :] * ext_ref[SUBLANES * a:SUBLANES * a + rows, :]
            zb = term if zb is None else zb + term
        y = y + zb[first + b:first + b + tm, :]
        if b < len(matmul_items):
            matmul_items[b]()
    ext_ref[0:HALO, :] = ext_ref[tm:tm + HALO, :]
    for r in range(0, tm, ROW_GROUP):
        cn_ref[0, r:r + ROW_GROUP, :] = _norm_conv_branch(y[r:r + ROW_GROUP], gng_ref, gnb_ref,
                                                          gout_ref, gmat_ref)


def _norm_conv_branch(y, gng_ref, gnb_ref, gout_ref, gmat_ref):
    gmat = gmat_ref[...]
    yh, yl = _split2(y)
    mu = _dot(yh, gmat) + _dot(yl, gmat)
    d = y - mu
    dh, dl = _split2(d * d)
    var = _dot(dh, gmat) + _dot(dl, gmat)
    yn = d * lax.rsqrt(var + LN_EPS) * gng_ref[...] + gnb_ref[...]
    sw = yn * jax.nn.sigmoid(yn)
    ms = jnp.mean(sw * sw, axis=-1, keepdims=True)
    return (sw * lax.rsqrt(ms + LN_EPS) * gout_ref[...]).astype(BF16)


def _decay_routing_matrices():
    selq_t = np.zeros((N_HEADS * AUG_ROWS, LANES), np.float32)
    selk = np.zeros((LANES, N_PAIRS * LANES), np.float32)
    for head in range(N_HEADS):
        p, j = divmod(head, 2)
        qbase = head * AUG_ROWS + AUG_PER_HEAD * j
        kbase = p * LANES + AUG_PER_HEAD * j
        for i in range(3):
            selq_t[qbase + i, PIECE_LANE[i] + head] = 1.0
            selq_t[qbase + 3 + i, ONE_LANE] = 1.0
            selk[ONE_LANE, kbase + i] = 1.0
            selk[PIECE_LANE[i] + head, kbase + 3 + i] = -1.0
    return jnp.asarray(selq_t, BF16), jnp.asarray(selk, BF16)


def _inproj(x, mod1, wqv_t, wk, wf, bf, wa, wg, pselq_t, pselk, *conv_params):
    b, s, d = x.shape
    tm = TM_PROJ
    const = lambda shape: pl.BlockSpec(shape, lambda bi, i: (0,) * len(shape))
    conv_specs = [const(p.shape) for p in conv_params]
    return pl.pallas_call(
        _inproj_kernel,
        out_shape=(jax.ShapeDtypeStruct((b, ATTN_WIDTH, s), BF16),
                   jax.ShapeDtypeStruct((b, s // TK, ATTN_WIDTH, TK), BF16),
                   jax.ShapeDtypeStruct((b, s, ATTN_WIDTH), BF16),
                   jax.ShapeDtypeStruct((b, N_HEADS * AUG_ROWS, s), BF16),
                   jax.ShapeDtypeStruct((b, s, N_PAIRS * LANES), BF16),
                   jax.ShapeDtypeStruct((b, s, CONV_WIDTH), BF16)),
        grid=(b, s // tm),
        in_specs=[pl.BlockSpec((1, tm, d), lambda bi, i: (bi, i, 0)),
                  pl.BlockSpec((1, 3, d), lambda bi, i: (bi, 0, 0)),
                  const(wqv_t.shape), const(wk.shape), const(wf.shape), const(bf.shape),
                  const(wa.shape), const(wg.shape), const(pselq_t.shape), const(pselk.shape)]
                 + conv_specs,
        out_specs=(pl.BlockSpec((1, ATTN_WIDTH, tm), lambda bi, i: (bi, 0, i)),
                   pl.BlockSpec((1, tm // TK, ATTN_WIDTH, TK), lambda bi, i: (bi, i, 0, 0)),
                   pl.BlockSpec((1, tm, ATTN_WIDTH), lambda bi, i: (bi, i, 0)),
                   pl.BlockSpec((1, N_HEADS * AUG_ROWS, tm), lambda bi, i: (bi, 0, i)),
                   pl.BlockSpec((1, tm, N_PAIRS * LANES), lambda bi, i: (bi, i, 0)),
                   pl.BlockSpec((1, tm, CONV_WIDTH), lambda bi, i: (bi, i, 0))),
        scratch_shapes=[pltpu.VMEM((1, LANES), F32),
                        pltpu.VMEM((HALO + tm + EXT_TAIL, CONV_WIDTH), F32)],
        compiler_params=pltpu.CompilerParams(
            dimension_semantics=("arbitrary", "arbitrary"), vmem_limit_bytes=VMEM_LIMIT),
        name="inproj",
    )(x, mod1, wqv_t, wk, wf, bf, wa, wg, pselq_t, pselk, *conv_params)


def _attn_kernel(qt_ref, aqt_ref, k_ref, ak_ref, vt_ref, w1_ref, w2_ref, o_ref, w1_bf_ref, w2_bf_ref,
                 s_scr, acc_scr, st_scr):
    @pl.when(pl.program_id(0) == 0)
    def _():
        w1_bf_ref[...] = w1_ref[...].astype(BF16)

    @pl.when(pl.program_id(0) == 1)
    def _():
        w2_bf_ref[...] = w2_ref[...].astype(BF16)

    for sub in range(TILES_PER_STEP):
        cols = slice(sub * TQ, (sub + 1) * TQ)
        _attn_tile(pl.program_id(2) * TILES_PER_STEP + sub,
                   qt_ref.at[0, :, cols], aqt_ref.at[0, :, cols], k_ref, ak_ref, vt_ref,
                   o_ref.at[0, cols, :], s_scr.at[sub], acc_scr.at[sub], st_scr.at[sub])


def _attn_tile(qi, qt_ref, aqt_ref, k_ref, ak_ref, vt_ref, o_ref, s_scr, acc_scr, st_scr):
    tq = qt_ref.shape[1]
    assert tq == TK, "the drain handles exactly one diagonal key block"
    heads = range(HEADS_PER_STEP)

    zeros_half = jnp.zeros((HEAD_DIM, tq), BF16)
    zeros_tail = jnp.zeros((LANES - AUG_ROWS, tq), BF16)
    qcat_t = []
    for h in heads:
        qh = qt_ref[HEAD_DIM * h:HEAD_DIM * (h + 1), :]
        halves = [qh, zeros_half] if h % 2 == 0 else [zeros_half, qh]
        qcat_t.append(jnp.concatenate(
            halves + [aqt_ref[AUG_ROWS * h:AUG_ROWS * (h + 1), :], zeros_tail], axis=0))

    def logits(kb, h):
        start = pl.multiple_of(kb * TK, TK)
        pair = pl.ds((h // 2) * LANES, LANES)
        kcat = jnp.concatenate([k_ref[0, pl.ds(start, TK), pair], ak_ref[0, pl.ds(start, TK), pair]],
                               axis=-1)
        return _dot(kcat, qcat_t[h])

    def block_max(s):
        return jnp.max(s, axis=0, keepdims=True)

    def softmax_update(s, m_blk, m):
        m_new = jnp.maximum(m, m_blk)
        alpha = jnp.exp2(m - m_new)
        p = jnp.exp2((s - m_new).astype(BF16))
        return p, alpha, m_new

    ones_rows = jnp.ones((AUG_ROWS, TK), BF16)

    def accumulate(kb, h, p, alpha, acc):
        vt = jnp.concatenate([vt_ref[0, kb, HEAD_DIM * h:HEAD_DIM * (h + 1), :], ones_rows], axis=0)
        return alpha * acc + _dot(vt, p)

    def stat(h, j):
        return st_scr.at[pl.ds(2 * h + j, 1), :]

    for h in heads:
        s0 = logits(0, h)
        s_scr[h] = s0
        acc_scr[h] = jnp.zeros((HEAD_DIM + AUG_ROWS, tq), F32)
        stat(h, 0)[...] = jnp.full((1, tq), NEG_BIG, F32)
        stat(h, 1)[...] = block_max(s0)

    def step(kb):
        for h in heads:
            s_next = logits(kb + 1, h)
            p, alpha, m = softmax_update(s_scr[h], stat(h, 1)[...], stat(h, 0)[...])
            acc_scr[h] = accumulate(kb, h, p, alpha, acc_scr[h])
            s_scr[h] = s_next
            stat(h, 0)[...] = m
            stat(h, 1)[...] = block_max(s_next)

    done = 0
    for unroll in KV_UNROLLS:
        trips = (qi - done) // unroll

        def body(i, carry, unroll=unroll, base=done):
            for u in range(unroll):
                step(base + unroll * i + u)
            return carry

        lax.fori_loop(0, trips, body, 0)
        done = done + trips * unroll

    key = lax.broadcasted_iota(jnp.int32, (TK, tq), 0)
    qry = lax.broadcasted_iota(jnp.int32, (TK, tq), 1)
    outs = []
    for h in heads:
        s = jnp.where(key <= qry, s_scr[h], NEG_BIG)
        p, alpha, _ = softmax_update(s, block_max(s), stat(h, 0)[...])
        acc = accumulate(qi, h, p, alpha, acc_scr[h])
        outs.append(acc[:HEAD_DIM] / acc[HEAD_DIM:HEAD_DIM + 1])
    for pr in range(HEADS_PER_STEP // 2):
        o_ref[:, pr * LANES:(pr + 1) * LANES] = jnp.concatenate(outs[2 * pr:2 * pr + 2], axis=0).T


def _attn(qt, augq_t, k, augk, vt, w1, w2):
    b, s, _ = k.shape
    g, n = HEADS_PER_STEP, TILES_PER_STEP
    tq_step = n * TQ
    steps = s // tq_step
    assert b == 2 and N_HEADS == g and w1.shape[0] % steps == 0 and w2.shape[0] % steps == 0
    w1_spec = pl.BlockSpec((w1.shape[0] // steps, w1.shape[1]),
                           lambda bi, p, i: (jnp.where(bi == 0, i, steps - 1), 0))
    w2_spec = pl.BlockSpec((w2.shape[0] // steps, w2.shape[1]),
                           lambda bi, p, i: (jnp.where(bi == 1, i, 0), 0))
    return pl.pallas_call(
        _attn_kernel,
        out_shape=(jax.ShapeDtypeStruct((b, s, ATTN_WIDTH), F32),
                   jax.ShapeDtypeStruct(w1.shape, BF16), jax.ShapeDtypeStruct(w2.shape, BF16)),
        grid=(b, N_HEADS // g, s // tq_step),
        in_specs=[pl.BlockSpec((1, g * HEAD_DIM, tq_step), lambda bi, p, i: (bi, p, i)),
                  pl.BlockSpec((1, g * AUG_ROWS, tq_step), lambda bi, p, i: (bi, p, i)),
                  pl.BlockSpec((1, s, g * HEAD_DIM), lambda bi, p, i: (bi, 0, p),
                               pipeline_mode=pl.Buffered(1)),
                  pl.BlockSpec((1, s, g * HEAD_DIM), lambda bi, p, i: (bi, 0, p),
                               pipeline_mode=pl.Buffered(1)),
                  pl.BlockSpec((1, s // TK, g * HEAD_DIM, TK), lambda bi, p, i: (bi, 0, p, 0),
                               pipeline_mode=pl.Buffered(1)),
                  w1_spec, w2_spec],
        out_specs=(pl.BlockSpec((1, tq_step, g * HEAD_DIM), lambda bi, p, i: (bi, i, p)),
                   w1_spec, w2_spec),
        scratch_shapes=[pltpu.VMEM((n, g, TK, TQ), F32),
                        pltpu.VMEM((n, g, HEAD_DIM + AUG_ROWS, TQ), F32),
                        pltpu.VMEM((n, 2 * g, TQ), F32)],
        compiler_params=pltpu.CompilerParams(
            dimension_semantics=("arbitrary", "arbitrary", "arbitrary"),
            vmem_limit_bytes=VMEM_LIMIT),
        name="attn",
    )(qt, augq_t, k, augk, vt, w1, w2)


def _layernorm(y, g, b):
    mu = jnp.mean(y, axis=-1, keepdims=True)
    d = y - mu
    var = jnp.mean(d * d, axis=-1, keepdims=True)
    return d * lax.rsqrt(var + LN_EPS) * g + b


def _mix_mlp_kernel(attn_ref, cn_ref, x_ref, mod_ref, gattn_ref, wo_a_ref, wo_c_ref,
                    ln1g_ref, ln1b_ref, w1_ref, w2_ref, ln2g_ref, ln2b_ref, o_ref):
    tm = x_ref.shape[1]
    groups = [slice(r, r + ROW_GROUP) for r in range(0, tm, ROW_GROUP)]
    gate1, shift2, scale2, gate2 = (mod_ref[0, j:j + 1, :] for j in range(2, 6))

    def token_mix(rows):
        a = attn_ref[0, rows, :]
        ms = jnp.mean(a * a, axis=-1, keepdims=True)
        an = (a * lax.rsqrt(ms + LN_EPS) * gattn_ref[...]).astype(BF16)
        mixed = _dot(an, wo_a_ref[...]) + _dot(cn_ref[0, rows, :], wo_c_ref[...])
        return DEEPNORM_ALPHA * x_ref[0, rows, :] + (1.0 + gate1) * mixed

    def mlp(x1):
        u = (x1 * (1.0 + scale2) + shift2).astype(BF16)
        ff = jnp.zeros(x1.shape, F32)
        for c in range(0, D_FF, FF_CHUNK):
            hid = jnp.maximum(_dot(u, w1_ref[:, c:c + FF_CHUNK]), 0.0)
            ff = ff + _dot((hid * hid).astype(BF16), w2_ref[c:c + FF_CHUNK, :])
        return DEEPNORM_ALPHA * x1 + (1.0 + gate2) * ff

    x1 = [_layernorm(token_mix(rows), ln1g_ref[...], ln1b_ref[...]) for rows in groups]
    for rows, x1_rows in zip(groups, x1):
        o_ref[0, rows, :] = _layernorm(mlp(x1_rows), ln2g_ref[...], ln2b_ref[...])


def _mix_mlp(attn, cn, x, mods, gattn, wo_a, wo_c, ln1g, ln1b, w1, w2, ln2g, ln2b):
    b, s, d = x.shape
    tm = TM_PROJ
    const = lambda a: pl.BlockSpec(a.shape, lambda bi, i: (0,) * a.ndim, pipeline_mode=pl.Buffered(1))
    return pl.pallas_call(
        _mix_mlp_kernel,
        out_shape=jax.ShapeDtypeStruct((b, s, d), F32),
        grid=(b, s // tm),
        in_specs=[pl.BlockSpec((1, tm, ATTN_WIDTH), lambda bi, i: (bi, i, 0)),
                  pl.BlockSpec((1, tm, CONV_WIDTH), lambda bi, i: (bi, i, 0)),
                  pl.BlockSpec((1, tm, d), lambda bi, i: (bi, i, 0)),
                  pl.BlockSpec((1, 6, d), lambda bi, i: (bi, 0, 0)),
                  const(gattn), const(wo_a), const(wo_c), const(ln1g), const(ln1b),
                  const(w1), const(w2), const(ln2g), const(ln2b)],
        out_specs=pl.BlockSpec((1, tm, d), lambda bi, i: (bi, i, 0)),
        compiler_params=pltpu.CompilerParams(
            dimension_semantics=("arbitrary", "arbitrary"), vmem_limit_bytes=VMEM_LIMIT),
        name="mix_mlp",
    )(attn, cn, x, mods, gattn, wo_a, wo_c, ln1g, ln1b, w1, w2, ln2g, ln2b)


def kernel(x, c, w_ada, b_ada, w_in, b_forget, w_dw, b_dw, gn_g, gn_b, g_attn_out, g_conv_out,
           w_out, ln1_g, ln1_b, w_ff1, w_ff2, ln2_g, ln2_b):
    bsz = x.shape[0]
    layer = 0
    row = lambda v: v.reshape(1, -1)

    c_pad = jnp.pad(c, ((0, 2 * SUBLANES - bsz), (0, 0)))
    ada = _ada(c_pad, w_ada[layer], row(b_ada[layer]))[:bsz]
    mods = ada.reshape(bsz, 6, D_MODEL)
    mod1 = mods[:, 0:3]

    w = w_in[layer]
    a0 = 3 * ATTN_WIDTH + N_HEADS
    wq, wk, wv = (w[:, i * ATTN_WIDTH:(i + 1) * ATTN_WIDTH] for i in range(3))
    wqv_t = jnp.concatenate([wq, wv], axis=1).T.astype(BF16)
    wk = wk.astype(BF16)
    reps = LANES // N_HEADS
    wf = jnp.tile(w[:, 3 * ATTN_WIDTH:a0], (1, reps)).astype(BF16)
    bf = jnp.tile(b_forget[layer], reps).reshape(1, LANES)
    wa = w[:, a0:a0 + CONV_WIDTH].astype(BF16)
    wg = w[:, a0 + CONV_WIDTH:].astype(BF16)

    pselq_t, pselk = _decay_routing_matrices()
    grp = np.arange(CONV_WIDTH) // CONV_GROUP
    gmat = jnp.asarray((grp[:, None] == grp[None, :]).astype(np.float32) / CONV_GROUP, BF16)
    qt, vt, k, augq_t, augk, cn = _inproj(
        x, mod1, wqv_t, wk, wf, bf, wa, wg, pselq_t, pselk,
        w_dw[layer].reshape(CONV_KERNEL, CONV_WIDTH), row(b_dw[layer]),
        row(gn_g[layer]), row(gn_b[layer]), row(g_conv_out[layer]), gmat)

    attn, w1_bf, w2_bf = _attn(qt, augq_t, k, augk, vt, w_ff1[layer], w_ff2[layer])

    wo = w_out[layer].astype(BF16)
    return _mix_mlp(attn, cn, x, mods, row(g_attn_out[layer]), wo[:ATTN_WIDTH], wo[ATTN_WIDTH:],
                    row(ln1_g[layer]), row(ln1_b[layer]),
                    w1_bf, w2_bf,
                    row(ln2_g[layer]), row(ln2_b[layer]))
```

```python
import functools

import numpy as np
import jax
import jax.numpy as jnp
from jax import lax
from jax.experimental import pallas as pl
from jax.experimental.pallas import tpu as pltpu

D_MODEL = 1024
HEAD_DIM = 64
ATTN_WIDTH = 512
CONV_WIDTH = 512
N_HEADS = 8
N_PAIRS = N_HEADS // 2
CONV_KERNEL = 31
CONV_GROUP = 64
D_FF = 4 * D_MODEL
LN_EPS = 1e-5
DEEPNORM_ALPHA = 2.0 ** 0.25

LANES = 128
SUBLANES = 8
HALO = 32
EXT_TAIL = 8
AUG_PER_HEAD = 6
AUG_ROWS = 16
NT_DIMS = (((1,), (1,)), ((), ()))
PIECE_LANE = (0, 8, 16)
ONE_LANE = 24
NEG_BIG = -1e30

TM_PROJ = 512
ROW_GROUP = 256
FF_CHUNK = 1024
TQ = 256
TK = 256
HEADS_PER_STEP = 8
KV_UNROLLS = (4, 2, 1)
TILES_PER_STEP = 2
LOG2E = 1.4426950408889634
VMEM_LIMIT = 56 * 1024 * 1024

F32 = jnp.float32
BF16 = jnp.bfloat16


def _split3(x):
    hi = x.astype(BF16)
    r = x - hi.astype(F32)
    mid = r.astype(BF16)
    lo = (r - mid.astype(F32)).astype(BF16)
    return hi, mid, lo


def _split2(x):
    hi = x.astype(BF16)
    lo = (x - hi.astype(F32)).astype(BF16)
    return hi, lo


def _dot(a, b):
    return jnp.dot(a, b, preferred_element_type=F32)


def _ada_kernel(c_ref, w_ref, b_ref, o_ref):
    c = c_ref[...]
    sh, sl = _split2(c * jax.nn.sigmoid(c))
    wh, wl = _split2(w_ref[...])
    o_ref[...] = (_dot(sh, wh) + (_dot(sl, wh) + _dot(sh, wl))) + b_ref[...]


def _ada(c_pad, w_ada, b_ada):
    n = w_ada.shape[1]
    tn = 1536
    return pl.pallas_call(
        _ada_kernel,
        out_shape=jax.ShapeDtypeStruct((c_pad.shape[0], n), F32),
        grid=(n // tn,),
        in_specs=[pl.BlockSpec(c_pad.shape, lambda j: (0, 0)),
                  pl.BlockSpec((D_MODEL, tn), lambda j: (0, j)),
                  pl.BlockSpec((1, tn), lambda j: (0, j))],
        out_specs=pl.BlockSpec((c_pad.shape[0], tn), lambda j: (0, j)),
        name="ada",
    )(c_pad, w_ada, b_ada)


def _inproj_kernel(x_ref, mod_ref, wqv_t_ref, wk_ref, wf_ref, bf_ref, wa_ref, wg_ref,
                   pselq_t_ref, pselk_ref, wdw_ref, bdw_ref, gng_ref, gnb_ref, gout_ref, gmat_ref,
                   w1_ref, w2_ref,
                   qt_ref, vt_ref, k_ref, augq_t_ref, augk_ref, cn_ref, w1_bf_ref, w2_bf_ref,
                   carry_ref, ext_ref):
    tm = x_ref.shape[1]

    @pl.when(pl.program_id(0) == 0)
    def _():
        w1_bf_ref[...] = w1_ref[...].astype(BF16)

    @pl.when(pl.program_id(0) == 1)
    def _():
        w2_bf_ref[...] = w2_ref[...].astype(BF16)

    @pl.when(pl.program_id(1) == 0)
    def _():
        carry_ref[...] = jnp.zeros_like(carry_ref)
        ext_ref[0:HALO, :] = jnp.zeros((HALO, CONV_WIDTH), F32)
        ext_ref[HALO + tm:, :] = jnp.zeros((EXT_TAIL, CONV_WIDTH), F32)

    x = x_ref[0]
    shift = mod_ref[0, 0:1, :]
    scale = mod_ref[0, 1:2, :]
    u = (x * (1.0 + scale) + shift).astype(BF16)

    ext_ref[HALO:HALO + tm, :] = _dot(u, wa_ref[...]) * jax.nn.sigmoid(_dot(u, wg_ref[...]))

    def qv_rows(lo, hi):
        return lax.dot_general(wqv_t_ref[lo:hi, :], u, NT_DIMS, preferred_element_type=F32)

    def emit_q(lo, hi):
        qt_ref[0, lo:hi, :] = (qv_rows(lo, hi) * (LOG2E * HEAD_DIM ** -0.5)).astype(BF16)

    def emit_v(lo, hi):
        v_t = qv_rows(ATTN_WIDTH + lo, ATTN_WIDTH + hi).astype(BF16)
        for t in range(tm // TK):
            vt_ref[0, t, lo:hi, :] = v_t[:, t * TK:(t + 1) * TK]

    def emit_k():
        k_ref[0] = _dot(u, wk_ref[...]).astype(BF16)

    def emit_decay():
        fl = _dot(u, wf_ref[...]) + bf_ref[...]
        log_f = jnp.minimum(fl, 0.0) - jnp.log(1.0 + jnp.exp(-jnp.abs(fl)))
        row = lax.broadcasted_iota(jnp.int32, (tm, tm), 0)
        col = lax.broadcasted_iota(jnp.int32, (tm, tm), 1)
        tri = jnp.where(row >= col, 1.0, 0.0).astype(BF16)
        h, m, l = _split3(log_f)
        cum = carry_ref[...] + ((_dot(tri, h) + _dot(tri, m)) + _dot(tri, l))
        carry_ref[...] = cum[tm - 1:tm, :]
        lane = lax.broadcasted_iota(jnp.int32, (tm, LANES), 1)
        ch, cm, cl = (t.astype(F32) for t in _split3(cum * LOG2E))
        pieces = jnp.where(lane < PIECE_LANE[1], ch,
                           jnp.where(lane < PIECE_LANE[2], cm,
                                     jnp.where(lane < ONE_LANE, cl,
                                               jnp.where(lane == ONE_LANE, 1.0, 0.0))))
        pieces = pieces.astype(BF16)
        augk_ref[0] = _dot(pieces, pselk_ref[...]).astype(BF16)
        augq_t_ref[0] = lax.dot_general(pselq_t_ref[...], pieces, NT_DIMS,
                                        preferred_element_type=F32).astype(BF16)

    half = ATTN_WIDTH // 2
    matmul_items = [lambda: emit_q(0, half), lambda: emit_q(half, ATTN_WIDTH),
                    lambda: emit_v(0, half), lambda: emit_v(half, ATTN_WIDTH),
                    emit_k, emit_decay]

    first = HALO - (CONV_KERNEL - 1)
    rows = tm + 2 * SUBLANES
    y = jnp.zeros((tm, CONV_WIDTH), F32) + bdw_ref[...]
    for b in range(SUBLANES):
        zb = None
        for a in range(pl.cdiv(CONV_KERNEL - b, SUBLANES)):
            k = SUBLANES * a + b
            term = wdw_ref[k:k + 1, :] * ext_ref[SUBLANES * a:SUBLANES * a + rows, :]
            zb = term if zb is None else zb + term
        y = y + zb[first + b:first + b + tm, :]
        if b < len(matmul_items):
            matmul_items[b]()
    ext_ref[0:HALO, :] = ext_ref[tm:tm + HALO, :]
    for r in range(0, tm, ROW_GROUP):
        cn_ref[0, r:r + ROW_GROUP, :] = _norm_conv_branch(y[r:r + ROW_GROUP], gng_ref, gnb_ref,
                                                          gout_ref, gmat_ref)


def _norm_conv_branch(y, gng_ref, gnb_ref, gout_ref, gmat_ref):
    gmat = gmat_ref[...]
    yh, yl = _split2(y)
    mu = _dot(yh, gmat) + _dot(yl, gmat)
    d = y - mu
    dh, dl = _split2(d * d)
    var = _dot(dh, gmat) + _dot(dl, gmat)
    yn = d * lax.rsqrt(var + LN_EPS) * gng_ref[...] + gnb_ref[...]
    sw = yn * jax.nn.sigmoid(yn)
    ms = jnp.mean(sw * sw, axis=-1, keepdims=True)
    return (sw * lax.rsqrt(ms + LN_EPS) * gout_ref[...]).astype(BF16)


def _decay_routing_matrices():
    selq_t = np.zeros((N_HEADS * AUG_ROWS, LANES), np.float32)
    selk = np.zeros((LANES, N_PAIRS * LANES), np.float32)
    for head in range(N_HEADS):
        p, j = divmod(head, 2)
        qbase = head * AUG_ROWS + AUG_PER_HEAD * j
        kbase = p * LANES + AUG_PER_HEAD * j
        for i in range(3):
            selq_t[qbase + i, PIECE_LANE[i] + head] = 1.0
            selq_t[qbase + 3 + i, ONE_LANE] = 1.0
            selk[ONE_LANE, kbase + i] = 1.0
            selk[PIECE_LANE[i] + head, kbase + 3 + i] = -1.0
    return jnp.asarray(selq_t, BF16), jnp.asarray(selk, BF16)


def _inproj(x, mod1, wqv_t, wk, wf, bf, wa, wg, pselq_t, pselk, conv_params, w1, w2):
    b, s, d = x.shape
    tm = TM_PROJ
    steps = s // tm
    const = lambda shape: pl.BlockSpec(shape, lambda bi, i: (0,) * len(shape))
    conv_specs = [const(p.shape) for p in conv_params]
    assert b == 2 and w1.shape[0] % steps == 0 and w2.shape[0] % steps == 0
    w1_spec = pl.BlockSpec((w1.shape[0] // steps, w1.shape[1]),
                           lambda bi, i: (jnp.where(bi == 0, i, steps - 1), 0))
    w2_spec = pl.BlockSpec((w2.shape[0] // steps, w2.shape[1]),
                           lambda bi, i: (jnp.where(bi == 1, i, 0), 0))
    return pl.pallas_call(
        _inproj_kernel,
        out_shape=(jax.ShapeDtypeStruct((b, ATTN_WIDTH, s), BF16),
                   jax.ShapeDtypeStruct((b, s // TK, ATTN_WIDTH, TK), BF16),
                   jax.ShapeDtypeStruct((b, s, ATTN_WIDTH), BF16),
                   jax.ShapeDtypeStruct((b, N_HEADS * AUG_ROWS, s), BF16),
                   jax.ShapeDtypeStruct((b, s, N_PAIRS * LANES), BF16),
                   jax.ShapeDtypeStruct((b, s, CONV_WIDTH), BF16),
                   jax.ShapeDtypeStruct(w1.shape, BF16), jax.ShapeDtypeStruct(w2.shape, BF16)),
        grid=(b, s // tm),
        in_specs=[pl.BlockSpec((1, tm, d), lambda bi, i: (bi, i, 0)),
                  pl.BlockSpec((1, 3, d), lambda bi, i: (bi, 0, 0)),
                  const(wqv_t.shape), const(wk.shape), const(wf.shape), const(bf.shape),
                  const(wa.shape), const(wg.shape), const(pselq_t.shape), const(pselk.shape)]
                 + conv_specs + [w1_spec, w2_spec],
        out_specs=(pl.BlockSpec((1, ATTN_WIDTH, tm), lambda bi, i: (bi, 0, i)),
                   pl.BlockSpec((1, tm // TK, ATTN_WIDTH, TK), lambda bi, i: (bi, i, 0, 0)),
                   pl.BlockSpec((1, tm, ATTN_WIDTH), lambda bi, i: (bi, i, 0)),
                   pl.BlockSpec((1, N_HEADS * AUG_ROWS, tm), lambda bi, i: (bi, 0, i)),
                   pl.BlockSpec((1, tm, N_PAIRS * LANES), lambda bi, i: (bi, i, 0)),
                   pl.BlockSpec((1, tm, CONV_WIDTH), lambda bi, i: (bi, i, 0)),
                   w1_spec, w2_spec),
        scratch_shapes=[pltpu.VMEM((1, LANES), F32),
                        pltpu.VMEM((HALO + tm + EXT_TAIL, CONV_WIDTH), F32)],
        compiler_params=pltpu.CompilerParams(
            dimension_semantics=("arbitrary", "arbitrary"), vmem_limit_bytes=VMEM_LIMIT),
        name="inproj",
    )(x, mod1, wqv_t, wk, wf, bf, wa, wg, pselq_t, pselk, *conv_params, w1, w2)


def _attn_kernel(qt_ref, aqt_ref, k_ref, ak_ref, vt_ref, o_ref, s_scr, acc_scr, st_scr):
    for sub in range(TILES_PER_STEP):
        cols = slice(sub * TQ, (sub + 1) * TQ)
        _attn_tile(pl.program_id(2) * TILES_PER_STEP + sub,
                   qt_ref.at[0, :, cols], aqt_ref.at[0, :, cols], k_ref, ak_ref, vt_ref,
                   o_ref.at[0, cols, :], s_scr.at[sub], acc_scr.at[sub], st_scr.at[sub])


def _attn_tile(qi, qt_ref, aqt_ref, k_ref, ak_ref, vt_ref, o_ref, s_scr, acc_scr, st_scr):
    tq = qt_ref.shape[1]
    assert tq == TK, "the drain handles exactly one diagonal key block"
    heads = range(HEADS_PER_STEP)

    zeros_half = jnp.zeros((HEAD_DIM, tq), BF16)
    zeros_tail = jnp.zeros((LANES - AUG_ROWS, tq), BF16)
    qcat_t = []
    for h in heads:
        qh = qt_ref[HEAD_DIM * h:HEAD_DIM * (h + 1), :]
        halves = [qh, zeros_half] if h % 2 == 0 else [zeros_half, qh]
        qcat_t.append(jnp.concatenate(
            halves + [aqt_ref[AUG_ROWS * h:AUG_ROWS * (h + 1), :], zeros_tail], axis=0))

    def logits(kb, h):
        start = pl.multiple_of(kb * TK, TK)
        pair = pl.ds((h // 2) * LANES, LANES)
        kcat = jnp.concatenate([k_ref[0, pl.ds(start, TK), pair], ak_ref[0, pl.ds(start, TK), pair]],
                               axis=-1)
        return _dot(kcat, qcat_t[h])

    def block_max(s):
        return jnp.max(s, axis=0, keepdims=True)

    def softmax_update(s, m_blk, m):
        m_new = jnp.maximum(m, m_blk)
        alpha = jnp.exp2(m - m_new)
        p = jnp.exp2((s - m_new).astype(BF16))
        return p, alpha, m_new

    ones_rows = jnp.ones((AUG_ROWS, TK), BF16)

    def accumulate(kb, h, p, alpha, acc):
        vt = jnp.concatenate([vt_ref[0, kb, HEAD_DIM * h:HEAD_DIM * (h + 1), :], ones_rows], axis=0)
        return alpha * acc + _dot(vt, p)

    def stat(h, j):
        return st_scr.at[pl.ds(2 * h + j, 1), :]

    for h in heads:
        s0 = logits(0, h)
        s_scr[h] = s0
        acc_scr[h] = jnp.zeros((HEAD_DIM + AUG_ROWS, tq), F32)
        stat(h, 0)[...] = jnp.full((1, tq), NEG_BIG, F32)
        stat(h, 1)[...] = block_max(s0)

    def step(kb):
        for h in heads:
            s_next = logits(kb + 1, h)
            p, alpha, m = softmax_update(s_scr[h], stat(h, 1)[...], stat(h, 0)[...])
            acc_scr[h] = accumulate(kb, h, p, alpha, acc_scr[h])
            s_scr[h] = s_next
            stat(h, 0)[...] = m
            stat(h, 1)[...] = block_max(s_next)

    done = 0
    for unroll in KV_UNROLLS:
        trips = (qi - done) // unroll

        def body(i, carry, unroll=unroll, base=done):
            for u in range(unroll):
                step(base + unroll * i + u)
            return carry

        lax.fori_loop(0, trips, body, 0)
        done = done + trips * unroll

    key = lax.broadcasted_iota(jnp.int32, (TK, tq), 0)
    qry = lax.broadcasted_iota(jnp.int32, (TK, tq), 1)
    outs = []
    for h in heads:
        s = jnp.where(key <= qry, s_scr[h], NEG_BIG)
        p, alpha, _ = softmax_update(s, block_max(s), stat(h, 0)[...])
        acc = accumulate(qi, h, p, alpha, acc_scr[h])
        outs.append(acc[:HEAD_DIM] / acc[HEAD_DIM:HEAD_DIM + 1])
    for pr in range(HEADS_PER_STEP // 2):
        o_ref[:, pr * LANES:(pr + 1) * LANES] = jnp.concatenate(outs[2 * pr:2 * pr + 2], axis=0).T


def _attn(qt, augq_t, k, augk, vt):
    b, s, _ = k.shape
    g, n = HEADS_PER_STEP, TILES_PER_STEP
    tq_step = n * TQ
    return pl.pallas_call(
        _attn_kernel,
        out_shape=jax.ShapeDtypeStruct((b, s, ATTN_WIDTH), F32),
        grid=(b, N_HEADS // g, s // tq_step),
        in_specs=[pl.BlockSpec((1, g * HEAD_DIM, tq_step), lambda bi, p, i: (bi, p, i)),
                  pl.BlockSpec((1, g * AUG_ROWS, tq_step), lambda bi, p, i: (bi, p, i)),
                  pl.BlockSpec((1, s, g * HEAD_DIM), lambda bi, p, i: (bi, 0, p),
                               pipeline_mode=pl.Buffered(1)),
                  pl.BlockSpec((1, s, g * HEAD_DIM), lambda bi, p, i: (bi, 0, p),
                               pipeline_mode=pl.Buffered(1)),
                  pl.BlockSpec((1, s // TK, g * HEAD_DIM, TK), lambda bi, p, i: (bi, 0, p, 0),
                               pipeline_mode=pl.Buffered(1))],
        out_specs=pl.BlockSpec((1, tq_step, g * HEAD_DIM), lambda bi, p, i: (bi, i, p)),
        scratch_shapes=[pltpu.VMEM((n, g, TK, TQ), F32),
                        pltpu.VMEM((n, g, HEAD_DIM + AUG_ROWS, TQ), F32),
                        pltpu.VMEM((n, 2 * g, TQ), F32)],
        compiler_params=pltpu.CompilerParams(
            dimension_semantics=("arbitrary", "arbitrary", "arbitrary"),
            vmem_limit_bytes=VMEM_LIMIT),
        name="attn",
    )(qt, augq_t, k, augk, vt)


def _layernorm(y, g, b):
    mu = jnp.mean(y, axis=-1, keepdims=True)
    d = y - mu
    var = jnp.mean(d * d, axis=-1, keepdims=True)
    return d * lax.rsqrt(var + LN_EPS) * g + b


def _mix_mlp_kernel(attn_ref, cn_ref, x_ref, mod_ref, gattn_ref, wo_a_ref, wo_c_ref,
                    ln1g_ref, ln1b_ref, w1_ref, w2_ref, ln2g_ref, ln2b_ref, o_ref):
    tm = x_ref.shape[1]
    groups = [slice(r, r + ROW_GROUP) for r in range(0, tm, ROW_GROUP)]
    gate1, shift2, scale2, gate2 = (mod_ref[0, j:j + 1, :] for j in range(2, 6))

    def token_mix(rows):
        a = attn_ref[0, rows, :]
        ms = jnp.mean(a * a, axis=-1, keepdims=True)
        an = (a * lax.rsqrt(ms + LN_EPS) * gattn_ref[...]).astype(BF16)
        mixed = _dot(an, wo_a_ref[...]) + _dot(cn_ref[0, rows, :], wo_c_ref[...])
        return DEEPNORM_ALPHA * x_ref[0, rows, :] + (1.0 + gate1) * mixed

    def mlp(x1):
        u = (x1 * (1.0 + scale2) + shift2).astype(BF16)
        ff = jnp.zeros(x1.shape, F32)
        for c in range(0, D_FF, FF_CHUNK):
            hid = jnp.maximum(_dot(u, w1_ref[:, c:c + FF_CHUNK]), 0.0)
            ff = ff + _dot((hid * hid).astype(BF16), w2_ref[c:c + FF_CHUNK, :])
        return DEEPNORM_ALPHA * x1 + (1.0 + gate2) * ff

    x1 = [_layernorm(token_mix(rows), ln1g_ref[...], ln1b_ref[...]) for rows in groups]
    for rows, x1_rows in zip(groups, x1):
        o_ref[0, rows, :] = _layernorm(mlp(x1_rows), ln2g_ref[...], ln2b_ref[...])


def _mix_mlp(attn, cn, x, mods, gattn, wo_a, wo_c, ln1g, ln1b, w1, w2, ln2g, ln2b):
    b, s, d = x.shape
    tm = TM_PROJ
    const = lambda a: pl.BlockSpec(a.shape, lambda bi, i: (0,) * a.ndim, pipeline_mode=pl.Buffered(1))
    return pl.pallas_call(
        _mix_mlp_kernel,
        out_shape=jax.ShapeDtypeStruct((b, s, d), F32),
        grid=(b, s // tm),
        in_specs=[pl.BlockSpec((1, tm, ATTN_WIDTH), lambda bi, i: (bi, i, 0)),
                  pl.BlockSpec((1, tm, CONV_WIDTH), lambda bi, i: (bi, i, 0)),
                  pl.BlockSpec((1, tm, d), lambda bi, i: (bi, i, 0)),
                  pl.BlockSpec((1, 6, d), lambda bi, i: (bi, 0, 0)),
                  const(gattn), const(wo_a), const(wo_c), const(ln1g), const(ln1b),
                  const(w1), const(w2), const(ln2g), const(ln2b)],
        out_specs=pl.BlockSpec((1, tm, d), lambda bi, i: (bi, i, 0)),
        compiler_params=pltpu.CompilerParams(
            dimension_semantics=("arbitrary", "arbitrary"), vmem_limit_bytes=VMEM_LIMIT),
        name="mix_mlp",
    )(attn, cn, x, mods, gattn, wo_a, wo_c, ln1g, ln1b, w1, w2, ln2g, ln2b)


def kernel(x, c, w_ada, b_ada, w_in, b_forget, w_dw, b_dw, gn_g, gn_b, g_attn_out, g_conv_out,
           w_out, ln1_g, ln1_b, w_ff1, w_ff2, ln2_g, ln2_b):
    bsz = x.shape[0]
    layer = 0
    row = lambda v: v.reshape(1, -1)

    c_pad = jnp.pad(c, ((0, 2 * SUBLANES - bsz), (0, 0)))
    ada = _ada(c_pad, w_ada[layer], row(b_ada[layer]))[:bsz]
    mods = ada.reshape(bsz, 6, D_MODEL)
    mod1 = mods[:, 0:3]

    w = w_in[layer]
    a0 = 3 * ATTN_WIDTH + N_HEADS
    wq, wk, wv = (w[:, i * ATTN_WIDTH:(i + 1) * ATTN_WIDTH] for i in range(3))
    wqv_t = jnp.concatenate([wq, wv], axis=1).T.astype(BF16)
    wk = wk.astype(BF16)
    reps = LANES // N_HEADS
    wf = jnp.tile(w[:, 3 * ATTN_WIDTH:a0], (1, reps)).astype(BF16)
    bf = jnp.tile(b_forget[layer], reps).reshape(1, LANES)
    wa = w[:, a0:a0 + CONV_WIDTH].astype(BF16)
    wg = w[:, a0 + CONV_WIDTH:].astype(BF16)

    pselq_t, pselk = _decay_routing_matrices()
    grp = np.arange(CONV_WIDTH) // CONV_GROUP
    gmat = jnp.asarray((grp[:, None] == grp[None, :]).astype(np.float32) / CONV_GROUP, BF16)
    conv_params = (w_dw[layer].reshape(CONV_KERNEL, CONV_WIDTH), row(b_dw[layer]),
                   row(gn_g[layer]), row(gn_b[layer]), row(g_conv_out[layer]), gmat)
    qt, vt, k, augq_t, augk, cn, w1_bf, w2_bf = _inproj(
        x, mod1, wqv_t, wk, wf, bf, wa, wg, pselq_t, pselk, conv_params, w_ff1[layer], w_ff2[layer])

    attn = _attn(qt, augq_t, k, augk, vt)

    wo = w_out[layer].astype(BF16)
    return _mix_mlp(attn, cn, x, mods, row(g_attn_out[layer]), wo[:ATTN_WIDTH], wo[ATTN_WIDTH:],
                    row(ln1_g[layer]), row(ln1_b[layer]),
                    w1_bf, w2_bf,
                    row(ln2_g[layer]), row(ln2_b[layer]))
```

```python
import functools

import numpy as np
import jax
import jax.numpy as jnp
from jax import lax
from jax.experimental import pallas as pl
from jax.experimental.pallas import tpu as pltpu

D_MODEL = 1024
HEAD_DIM = 64
ATTN_WIDTH = 512
CONV_WIDTH = 512
N_HEADS = 8
N_PAIRS = N_HEADS // 2
CONV_KERNEL = 31
CONV_GROUP = 64
D_FF = 4 * D_MODEL
LN_EPS = 1e-5
DEEPNORM_ALPHA = 2.0 ** 0.25

LANES = 128
SUBLANES = 8
HALO = 32
EXT_TAIL = 8
AUG_PER_HEAD = 6
AUG_ROWS = 16
NT_DIMS = (((1,), (1,)), ((), ()))
PIECE_LANE = (0, 8, 16)
ONE_LANE = 24
NEG_BIG = -1e30

TM_PROJ = 1024
TM_MLP = 512
CUMSUM_ROWS = 512
ROW_GROUP = 256
FF_CHUNK = 1024
TQ = 256
TK = 256
HEADS_PER_STEP = 8
KV_UNROLLS = (4, 2, 1)
TILES_PER_STEP = 2
LOG2E = 1.4426950408889634
VMEM_LIMIT = 56 * 1024 * 1024

F32 = jnp.float32
BF16 = jnp.bfloat16


def _split3(x):
    hi = x.astype(BF16)
    r = x - hi.astype(F32)
    mid = r.astype(BF16)
    lo = (r - mid.astype(F32)).astype(BF16)
    return hi, mid, lo


def _split2(x):
    hi = x.astype(BF16)
    lo = (x - hi.astype(F32)).astype(BF16)
    return hi, lo


def _dot(a, b):
    return jnp.dot(a, b, preferred_element_type=F32)


def _ada_kernel(c_ref, w_ref, b_ref, o_ref):
    c = c_ref[...]
    sh, sl = _split2(c * jax.nn.sigmoid(c))
    wh, wl = _split2(w_ref[...])
    o_ref[...] = (_dot(sh, wh) + (_dot(sl, wh) + _dot(sh, wl))) + b_ref[...]


def _ada(c_pad, w_ada, b_ada):
    n = w_ada.shape[1]
    tn = 1536
    return pl.pallas_call(
        _ada_kernel,
        out_shape=jax.ShapeDtypeStruct((c_pad.shape[0], n), F32),
        grid=(n // tn,),
        in_specs=[pl.BlockSpec(c_pad.shape, lambda j: (0, 0)),
                  pl.BlockSpec((D_MODEL, tn), lambda j: (0, j)),
                  pl.BlockSpec((1, tn), lambda j: (0, j))],
        out_specs=pl.BlockSpec((c_pad.shape[0], tn), lambda j: (0, j)),
        name="ada",
    )(c_pad, w_ada, b_ada)


def _inproj_kernel(x_ref, mod_ref, wqv_t_ref, wk_ref, wf_ref, bf_ref, wa_ref, wg_ref,
                   pselq_t_ref, pselk_ref, wdw_ref, bdw_ref, gng_ref, gnb_ref, gout_ref, gmat_ref,
                   w1_ref, w2_ref,
                   qt_ref, vt_ref, k_ref, augq_t_ref, augk_ref, cn_ref, w1_bf_ref, w2_bf_ref,
                   carry_ref, ext_ref):
    tm = x_ref.shape[1]

    @pl.when(pl.program_id(0) == 0)
    def _():
        w1_bf_ref[...] = w1_ref[...].astype(BF16)

    @pl.when(pl.program_id(0) == 1)
    def _():
        w2_bf_ref[...] = w2_ref[...].astype(BF16)

    @pl.when(pl.program_id(1) == 0)
    def _():
        carry_ref[...] = jnp.zeros_like(carry_ref)
        ext_ref[0:HALO, :] = jnp.zeros((HALO, CONV_WIDTH), F32)
        ext_ref[HALO + tm:, :] = jnp.zeros((EXT_TAIL, CONV_WIDTH), F32)

    x = x_ref[0]
    shift = mod_ref[0, 0:1, :]
    scale = mod_ref[0, 1:2, :]
    u = (x * (1.0 + scale) + shift).astype(BF16)

    ext_ref[HALO:HALO + tm, :] = _dot(u, wa_ref[...]) * jax.nn.sigmoid(_dot(u, wg_ref[...]))

    def qv_rows(lo, hi):
        return lax.dot_general(wqv_t_ref[lo:hi, :], u, NT_DIMS, preferred_element_type=F32)

    def emit_q(lo, hi):
        qt_ref[0, lo:hi, :] = (qv_rows(lo, hi) * (LOG2E * HEAD_DIM ** -0.5)).astype(BF16)

    def emit_v(lo, hi):
        v_t = qv_rows(ATTN_WIDTH + lo, ATTN_WIDTH + hi).astype(BF16)
        for t in range(tm // TK):
            vt_ref[0, t, lo:hi, :] = v_t[:, t * TK:(t + 1) * TK]

    def emit_k():
        k_ref[0] = _dot(u, wk_ref[...]).astype(BF16)

    def emit_decay():
        fl = _dot(u, wf_ref[...]) + bf_ref[...]
        log_f = jnp.minimum(fl, 0.0) - jnp.log(1.0 + jnp.exp(-jnp.abs(fl)))
        n = CUMSUM_ROWS
        row = lax.broadcasted_iota(jnp.int32, (n, n), 0)
        col = lax.broadcasted_iota(jnp.int32, (n, n), 1)
        tri = jnp.where(row >= col, 1.0, 0.0).astype(BF16)
        lane = lax.broadcasted_iota(jnp.int32, (n, LANES), 1)
        for r in range(0, tm, n):
            h, m, l = _split3(log_f[r:r + n])
            cum = carry_ref[...] + ((_dot(tri, h) + _dot(tri, m)) + _dot(tri, l))
            carry_ref[...] = cum[n - 1:n, :]
            ch, cm, cl = (t.astype(F32) for t in _split3(cum * LOG2E))
            pieces = jnp.where(lane < PIECE_LANE[1], ch,
                               jnp.where(lane < PIECE_LANE[2], cm,
                                         jnp.where(lane < ONE_LANE, cl,
                                                   jnp.where(lane == ONE_LANE, 1.0, 0.0))))
            pieces = pieces.astype(BF16)
            augk_ref[0, r:r + n, :] = _dot(pieces, pselk_ref[...]).astype(BF16)
            augq_t_ref[0, :, r:r + n] = lax.dot_general(
                pselq_t_ref[...], pieces, NT_DIMS, preferred_element_type=F32).astype(BF16)

    half = ATTN_WIDTH // 2
    matmul_items = [lambda: emit_q(0, half), lambda: emit_q(half, ATTN_WIDTH),
                    lambda: emit_v(0, half), lambda: emit_v(half, ATTN_WIDTH),
                    emit_k, emit_decay]

    first = HALO - (CONV_KERNEL - 1)
    rows = tm + 2 * SUBLANES
    y = jnp.zeros((tm, CONV_WIDTH), F32) + bdw_ref[...]
    for b in range(SUBLANES):
        zb = None
        for a in range(pl.cdiv(CONV_KERNEL - b, SUBLANES)):
            k = SUBLANES * a + b
            term = wdw_ref[k:k + 1, :] * ext_ref[SUBLANES * a:SUBLANES * a + rows, :]
            zb = term if zb is None else zb + term
        y = y + zb[first + b:first + b + tm, :]
        if b < len(matmul_items):
            matmul_items[b]()
    ext_ref[0:HALO, :] = ext_ref[tm:tm + HALO, :]
    for r in range(0, tm, ROW_GROUP):
        cn_ref[0, r:r + ROW_GROUP, :] = _norm_conv_branch(y[r:r + ROW_GROUP], gng_ref, gnb_ref,
                                                          gout_ref, gmat_ref)


def _norm_conv_branch(y, gng_ref, gnb_ref, gout_ref, gmat_ref):
    gmat = gmat_ref[...]
    yh, yl = _split2(y)
    mu = _dot(yh, gmat) + _dot(yl, gmat)
    d = y - mu
    dh, dl = _split2(d * d)
    var = _dot(dh, gmat) + _dot(dl, gmat)
    yn = d * lax.rsqrt(var + LN_EPS) * gng_ref[...] + gnb_ref[...]
    sw = yn * jax.nn.sigmoid(yn)
    ms = jnp.mean(sw * sw, axis=-1, keepdims=True)
    return (sw * lax.rsqrt(ms + LN_EPS) * gout_ref[...]).astype(BF16)


def _decay_routing_matrices():
    selq_t = np.zeros((N_HEADS * AUG_ROWS, LANES), np.float32)
    selk = np.zeros((LANES, N_PAIRS * LANES), np.float32)
    for head in range(N_HEADS):
        p, j = divmod(head, 2)
        qbase = head * AUG_ROWS + AUG_PER_HEAD * j
        kbase = p * LANES + AUG_PER_HEAD * j
        for i in range(3):
            selq_t[qbase + i, PIECE_LANE[i] + head] = 1.0
            selq_t[qbase + 3 + i, ONE_LANE] = 1.0
            selk[ONE_LANE, kbase + i] = 1.0
            selk[PIECE_LANE[i] + head, kbase + 3 + i] = -1.0
    return jnp.asarray(selq_t, BF16), jnp.asarray(selk, BF16)


def _inproj(x, mod1, wqv_t, wk, wf, bf, wa, wg, pselq_t, pselk, conv_params, w1, w2):
    b, s, d = x.shape
    tm = TM_PROJ
    steps = s // tm
    const = lambda shape: pl.BlockSpec(shape, lambda bi, i: (0,) * len(shape))
    conv_specs = [const(p.shape) for p in conv_params]
    assert b == 2 and w1.shape[0] % steps == 0 and w2.shape[0] % steps == 0
    w1_spec = pl.BlockSpec((w1.shape[0] // steps, w1.shape[1]),
                           lambda bi, i: (jnp.where(bi == 0, i, steps - 1), 0))
    w2_spec = pl.BlockSpec((w2.shape[0] // steps, w2.shape[1]),
                           lambda bi, i: (jnp.where(bi == 1, i, 0), 0))
    return pl.pallas_call(
        _inproj_kernel,
        out_shape=(jax.ShapeDtypeStruct((b, ATTN_WIDTH, s), BF16),
                   jax.ShapeDtypeStruct((b, s // TK, ATTN_WIDTH, TK), BF16),
                   jax.ShapeDtypeStruct((b, s, ATTN_WIDTH), BF16),
                   jax.ShapeDtypeStruct((b, N_HEADS * AUG_ROWS, s), BF16),
                   jax.ShapeDtypeStruct((b, s, N_PAIRS * LANES), BF16),
                   jax.ShapeDtypeStruct((b, s, CONV_WIDTH), BF16),
                   jax.ShapeDtypeStruct(w1.shape, BF16), jax.ShapeDtypeStruct(w2.shape, BF16)),
        grid=(b, s // tm),
        in_specs=[pl.BlockSpec((1, tm, d), lambda bi, i: (bi, i, 0)),
                  pl.BlockSpec((1, 3, d), lambda bi, i: (bi, 0, 0)),
                  const(wqv_t.shape), const(wk.shape), const(wf.shape), const(bf.shape),
                  const(wa.shape), const(wg.shape), const(pselq_t.shape), const(pselk.shape)]
                 + conv_specs + [w1_spec, w2_spec],
        out_specs=(pl.BlockSpec((1, ATTN_WIDTH, tm), lambda bi, i: (bi, 0, i)),
                   pl.BlockSpec((1, tm // TK, ATTN_WIDTH, TK), lambda bi, i: (bi, i, 0, 0)),
                   pl.BlockSpec((1, tm, ATTN_WIDTH), lambda bi, i: (bi, i, 0)),
                   pl.BlockSpec((1, N_HEADS * AUG_ROWS, tm), lambda bi, i: (bi, 0, i)),
                   pl.BlockSpec((1, tm, N_PAIRS * LANES), lambda bi, i: (bi, i, 0)),
                   pl.BlockSpec((1, tm, CONV_WIDTH), lambda bi, i: (bi, i, 0)),
                   w1_spec, w2_spec),
        scratch_shapes=[pltpu.VMEM((1, LANES), F32),
                        pltpu.VMEM((HALO + tm + EXT_TAIL, CONV_WIDTH), F32)],
        compiler_params=pltpu.CompilerParams(
            dimension_semantics=("arbitrary", "arbitrary"), vmem_limit_bytes=VMEM_LIMIT),
        name="inproj",
    )(x, mod1, wqv_t, wk, wf, bf, wa, wg, pselq_t, pselk, *conv_params, w1, w2)


def _attn_kernel(qt_ref, aqt_ref, k_ref, ak_ref, vt_ref, o_ref, s_scr, acc_scr, st_scr):
    for sub in range(TILES_PER_STEP):
        cols = slice(sub * TQ, (sub + 1) * TQ)
        _attn_tile(pl.program_id(2) * TILES_PER_STEP + sub,
                   qt_ref.at[0, :, cols], aqt_ref.at[0, :, cols], k_ref, ak_ref, vt_ref,
                   o_ref.at[0, cols, :], s_scr.at[sub], acc_scr.at[sub], st_scr.at[sub])


def _attn_tile(qi, qt_ref, aqt_ref, k_ref, ak_ref, vt_ref, o_ref, s_scr, acc_scr, st_scr):
    tq = qt_ref.shape[1]
    assert tq == TK, "the drain handles exactly one diagonal key block"
    heads = range(HEADS_PER_STEP)

    zeros_half = jnp.zeros((HEAD_DIM, tq), BF16)
    zeros_tail = jnp.zeros((LANES - AUG_ROWS, tq), BF16)
    qcat_t = []
    for h in heads:
        qh = qt_ref[HEAD_DIM * h:HEAD_DIM * (h + 1), :]
        halves = [qh, zeros_half] if h % 2 == 0 else [zeros_half, qh]
        qcat_t.append(jnp.concatenate(
            halves + [aqt_ref[AUG_ROWS * h:AUG_ROWS * (h + 1), :], zeros_tail], axis=0))

    def logits(kb, h):
        start = pl.multiple_of(kb * TK, TK)
        pair = pl.ds((h // 2) * LANES, LANES)
        kcat = jnp.concatenate([k_ref[0, pl.ds(start, TK), pair], ak_ref[0, pl.ds(start, TK), pair]],
                               axis=-1)
        return _dot(kcat, qcat_t[h])

    def block_max(s):
        return jnp.max(s, axis=0, keepdims=True)

    def softmax_update(s, m_blk, m):
        m_new = jnp.maximum(m, m_blk)
        alpha = jnp.exp2(m - m_new)
        p = jnp.exp2((s - m_new).astype(BF16))
        return p, alpha, m_new

    ones_rows = jnp.ones((AUG_ROWS, TK), BF16)

    def accumulate(kb, h, p, alpha, acc):
        vt = jnp.concatenate([vt_ref[0, kb, HEAD_DIM * h:HEAD_DIM * (h + 1), :], ones_rows], axis=0)
        return alpha * acc + _dot(vt, p)

    def stat(h, j):
        return st_scr.at[pl.ds(2 * h + j, 1), :]

    for h in heads:
        s0 = logits(0, h)
        s_scr[h] = s0
        acc_scr[h] = jnp.zeros((HEAD_DIM + AUG_ROWS, tq), F32)
        stat(h, 0)[...] = jnp.full((1, tq), NEG_BIG, F32)
        stat(h, 1)[...] = block_max(s0)

    def step(kb):
        for h in heads:
            s_next = logits(kb + 1, h)
            p, alpha, m = softmax_update(s_scr[h], stat(h, 1)[...], stat(h, 0)[...])
            acc_scr[h] = accumulate(kb, h, p, alpha, acc_scr[h])
            s_scr[h] = s_next
            stat(h, 0)[...] = m
            stat(h, 1)[...] = block_max(s_next)

    done = 0
    for unroll in KV_UNROLLS:
        trips = (qi - done) // unroll

        def body(i, carry, unroll=unroll, base=done):
            for u in range(unroll):
                step(base + unroll * i + u)
            return carry

        lax.fori_loop(0, trips, body, 0)
        done = done + trips * unroll

    key = lax.broadcasted_iota(jnp.int32, (TK, tq), 0)
    qry = lax.broadcasted_iota(jnp.int32, (TK, tq), 1)
    outs = []
    for h in heads:
        s = jnp.where(key <= qry, s_scr[h], NEG_BIG)
        p, alpha, _ = softmax_update(s, block_max(s), stat(h, 0)[...])
        acc = accumulate(qi, h, p, alpha, acc_scr[h])
        outs.append(acc[:HEAD_DIM] / acc[HEAD_DIM:HEAD_DIM + 1])
    for pr in range(HEADS_PER_STEP // 2):
        o_ref[:, pr * LANES:(pr + 1) * LANES] = jnp.concatenate(outs[2 * pr:2 * pr + 2], axis=0).T


def _attn(qt, augq_t, k, augk, vt):
    b, s, _ = k.shape
    g, n = HEADS_PER_STEP, TILES_PER_STEP
    tq_step = n * TQ
    return pl.pallas_call(
        _attn_kernel,
        out_shape=jax.ShapeDtypeStruct((b, s, ATTN_WIDTH), F32),
        grid=(b, N_HEADS // g, s // tq_step),
        in_specs=[pl.BlockSpec((1, g * HEAD_DIM, tq_step), lambda bi, p, i: (bi, p, i)),
                  pl.BlockSpec((1, g * AUG_ROWS, tq_step), lambda bi, p, i: (bi, p, i)),
                  pl.BlockSpec((1, s, g * HEAD_DIM), lambda bi, p, i: (bi, 0, p),
                               pipeline_mode=pl.Buffered(1)),
                  pl.BlockSpec((1, s, g * HEAD_DIM), lambda bi, p, i: (bi, 0, p),
                               pipeline_mode=pl.Buffered(1)),
                  pl.BlockSpec((1, s // TK, g * HEAD_DIM, TK), lambda bi, p, i: (bi, 0, p, 0),
                               pipeline_mode=pl.Buffered(1))],
        out_specs=pl.BlockSpec((1, tq_step, g * HEAD_DIM), lambda bi, p, i: (bi, i, p)),
        scratch_shapes=[pltpu.VMEM((n, g, TK, TQ), F32),
                        pltpu.VMEM((n, g, HEAD_DIM + AUG_ROWS, TQ), F32),
                        pltpu.VMEM((n, 2 * g, TQ), F32)],
        compiler_params=pltpu.CompilerParams(
            dimension_semantics=("arbitrary", "arbitrary", "arbitrary"),
            vmem_limit_bytes=VMEM_LIMIT),
        name="attn",
    )(qt, augq_t, k, augk, vt)


def _layernorm(y, g, b):
    mu = jnp.mean(y, axis=-1, keepdims=True)
    d = y - mu
    var = jnp.mean(d * d, axis=-1, keepdims=True)
    return d * lax.rsqrt(var + LN_EPS) * g + b


def _mix_mlp_kernel(attn_ref, cn_ref, x_ref, mod_ref, gattn_ref, wo_a_ref, wo_c_ref,
                    ln1g_ref, ln1b_ref, w1_ref, w2_ref, ln2g_ref, ln2b_ref, o_ref):
    tm = x_ref.shape[1]
    groups = [slice(r, r + ROW_GROUP) for r in range(0, tm, ROW_GROUP)]
    gate1, shift2, scale2, gate2 = (mod_ref[0, j:j + 1, :] for j in range(2, 6))

    def token_mix(rows):
        a = attn_ref[0, rows, :]
        ms = jnp.mean(a * a, axis=-1, keepdims=True)
        an = (a * lax.rsqrt(ms + LN_EPS) * gattn_ref[...]).astype(BF16)
        mixed = _dot(an, wo_a_ref[...]) + _dot(cn_ref[0, rows, :], wo_c_ref[...])
        return DEEPNORM_ALPHA * x_ref[0, rows, :] + (1.0 + gate1) * mixed

    def mlp(x1):
        u = (x1 * (1.0 + scale2) + shift2).astype(BF16)
        ff = jnp.zeros(x1.shape, F32)
        for c in range(0, D_FF, FF_CHUNK):
            hid = jnp.maximum(_dot(u, w1_ref[:, c:c + FF_CHUNK]), 0.0)
            ff = ff + _dot((hid * hid).astype(BF16), w2_ref[c:c + FF_CHUNK, :])
        return DEEPNORM_ALPHA * x1 + (1.0 + gate2) * ff

    x1 = [_layernorm(token_mix(rows), ln1g_ref[...], ln1b_ref[...]) for rows in groups]
    for rows, x1_rows in zip(groups, x1):
        o_ref[0, rows, :] = _layernorm(mlp(x1_rows), ln2g_ref[...], ln2b_ref[...])


def _mix_mlp(attn, cn, x, mods, gattn, wo_a, wo_c, ln1g, ln1b, w1, w2, ln2g, ln2b):
    b, s, d = x.shape
    tm = TM_MLP
    const = lambda a: pl.BlockSpec(a.shape, lambda bi, i: (0,) * a.ndim, pipeline_mode=pl.Buffered(1))
    return pl.pallas_call(
        _mix_mlp_kernel,
        out_shape=jax.ShapeDtypeStruct((b, s, d), F32),
        grid=(b, s // tm),
        in_specs=[pl.BlockSpec((1, tm, ATTN_WIDTH), lambda bi, i: (bi, i, 0)),
                  pl.BlockSpec((1, tm, CONV_WIDTH), lambda bi, i: (bi, i, 0)),
                  pl.BlockSpec((1, tm, d), lambda bi, i: (bi, i, 0)),
                  pl.BlockSpec((1, 6, d), lambda bi, i: (bi, 0, 0)),
                  const(gattn), const(wo_a), const(wo_c), const(ln1g), const(ln1b),
                  const(w1), const(w2), const(ln2g), const(ln2b)],
        out_specs=pl.BlockSpec((1, tm, d), lambda bi, i: (bi, i, 0)),
        compiler_params=pltpu.CompilerParams(
            dimension_semantics=("arbitrary", "arbitrary"), vmem_limit_bytes=VMEM_LIMIT),
        name="mix_mlp",
    )(attn, cn, x, mods, gattn, wo_a, wo_c, ln1g, ln1b, w1, w2, ln2g, ln2b)


def kernel(x, c, w_ada, b_ada, w_in, b_forget, w_dw, b_dw, gn_g, gn_b, g_attn_out, g_conv_out,
           w_out, ln1_g, ln1_b, w_ff1, w_ff2, ln2_g, ln2_b):
    bsz = x.shape[0]
    layer = 0
    row = lambda v: v.reshape(1, -1)

    c_pad = jnp.pad(c, ((0, 2 * SUBLANES - bsz), (0, 0)))
    ada = _ada(c_pad, w_ada[layer], row(b_ada[layer]))[:bsz]
    mods = ada.reshape(bsz, 6, D_MODEL)
    mod1 = mods[:, 0:3]

    w = w_in[layer]
    a0 = 3 * ATTN_WIDTH + N_HEADS
    wq, wk, wv = (w[:, i * ATTN_WIDTH:(i + 1) * ATTN_WIDTH] for i in range(3))
    wqv_t = jnp.concatenate([wq, wv], axis=1).T.astype(BF16)
    wk = wk.astype(BF16)
    reps = LANES // N_HEADS
    wf = jnp.tile(w[:, 3 * ATTN_WIDTH:a0], (1, reps)).astype(BF16)
    bf = jnp.tile(b_forget[layer], reps).reshape(1, LANES)
    wa = w[:, a0:a0 + CONV_WIDTH].astype(BF16)
    wg = w[:, a0 + CONV_WIDTH:].astype(BF16)

    pselq_t, pselk = _decay_routing_matrices()
    grp = np.arange(CONV_WIDTH) // CONV_GROUP
    gmat = jnp.asarray((grp[:, None] == grp[None, :]).astype(np.float32) / CONV_GROUP, BF16)
    conv_params = (w_dw[layer].reshape(CONV_KERNEL, CONV_WIDTH), row(b_dw[layer]),
                   row(gn_g[layer]), row(gn_b[layer]), row(g_conv_out[layer]), gmat)
    qt, vt, k, augq_t, augk, cn, w1_bf, w2_bf = _inproj(
        x, mod1, wqv_t, wk, wf, bf, wa, wg, pselq_t, pselk, conv_params, w_ff1[layer], w_ff2[layer])

    attn = _attn(qt, augq_t, k, augk, vt)

    wo = w_out[layer].astype(BF16)
    return _mix_mlp(attn, cn, x, mods, row(g_attn_out[layer]), wo[:ATTN_WIDTH], wo[ATTN_WIDTH:],
                    row(ln1_g[layer]), row(ln1_b[layer]),
                    w1_bf, w2_bf,
                    row(ln2_g[layer]), row(ln2_b[layer]))
```

```python
import functools

import numpy as np
import jax
import jax.numpy as jnp
from jax import lax
from jax.experimental import pallas as pl
from jax.experimental.pallas import tpu as pltpu

D_MODEL = 1024
HEAD_DIM = 64
ATTN_WIDTH = 512
CONV_WIDTH = 512
N_HEADS = 8
N_PAIRS = N_HEADS // 2
CONV_KERNEL = 31
CONV_GROUP = 64
D_FF = 4 * D_MODEL
LN_EPS = 1e-5
DEEPNORM_ALPHA = 2.0 ** 0.25

LANES = 128
SUBLANES = 8
HALO = 32
EXT_TAIL = 8
AUG_PER_HEAD = 6
AUG_ROWS = 16
NT_DIMS = (((1,), (1,)), ((), ()))
PIECE_LANE = (0, 8, 16)
ONE_LANE = 24
NEG_BIG = -1e30

TM_PROJ = 1024
TM_MLP = 512
ADA_TN = 768
ADA_BUFS = 3
CUMSUM_ROWS = 512
ROW_GROUP = 256
FF_CHUNK = 1024
TQ = 256
TK = 256
HEADS_PER_STEP = 8
KV_UNROLLS = (4, 2, 1)
TILES_PER_STEP = 2
LOG2E = 1.4426950408889634
VMEM_LIMIT = 56 * 1024 * 1024

F32 = jnp.float32
BF16 = jnp.bfloat16


def _split3(x):
    hi = x.astype(BF16)
    r = x - hi.astype(F32)
    mid = r.astype(BF16)
    lo = (r - mid.astype(F32)).astype(BF16)
    return hi, mid, lo


def _split2(x):
    hi = x.astype(BF16)
    lo = (x - hi.astype(F32)).astype(BF16)
    return hi, lo


def _dot(a, b):
    return jnp.dot(a, b, preferred_element_type=F32)


def _ada_kernel(c_ref, w_hbm, b_ref, o_ref, wbuf, sem):
    tn = wbuf.shape[2]
    n_chunks = o_ref.shape[1] // tn

    def copy(j):
        slot = j % ADA_BUFS
        return pltpu.make_async_copy(w_hbm.at[:, pl.ds(j * tn, tn)], wbuf.at[slot], sem.at[slot])

    for j in range(min(ADA_BUFS, n_chunks)):
        copy(j).start()
    c = c_ref[...]
    sh, sl = _split2(c * jax.nn.sigmoid(c))
    for j in range(n_chunks):
        cols = slice(j * tn, (j + 1) * tn)
        copy(j).wait()
        wh, wl = _split2(wbuf[j % ADA_BUFS])
        o_ref[:, cols] = (_dot(sh, wh) + (_dot(sl, wh) + _dot(sh, wl))) + b_ref[:, cols]
        if j + ADA_BUFS < n_chunks:
            copy(j + ADA_BUFS).start()


def _ada(c_pad, w_ada, b_ada):
    n = w_ada.shape[1]
    assert n % ADA_TN == 0
    return pl.pallas_call(
        _ada_kernel,
        out_shape=jax.ShapeDtypeStruct((c_pad.shape[0], n), F32),
        in_specs=[pl.BlockSpec(memory_space=pltpu.VMEM), pl.BlockSpec(memory_space=pl.ANY),
                  pl.BlockSpec(memory_space=pltpu.VMEM)],
        out_specs=pl.BlockSpec(memory_space=pltpu.VMEM),
        scratch_shapes=[pltpu.VMEM((ADA_BUFS, D_MODEL, ADA_TN), F32),
                        pltpu.SemaphoreType.DMA((ADA_BUFS,))],
        name="ada",
    )(c_pad, w_ada, b_ada)


def _inproj_kernel(x_ref, mod_ref, wqv_t_ref, wk_ref, wf_ref, bf_ref, wa_ref, wg_ref,
                   pselq_t_ref, pselk_ref, wdw_ref, bdw_ref, gng_ref, gnb_ref, gout_ref, gmat_ref,
                   w1_ref, w2_ref,
                   qt_ref, vt_ref, k_ref, augq_t_ref, augk_ref, cn_ref, w1_bf_ref, w2_bf_ref,
                   carry_ref, ext_ref):
    tm = x_ref.shape[1]

    @pl.when(pl.program_id(0) == 0)
    def _():
        w1_bf_ref[...] = w1_ref[...].astype(BF16)

    @pl.when(pl.program_id(0) == 1)
    def _():
        w2_bf_ref[...] = w2_ref[...].astype(BF16)

    @pl.when(pl.program_id(1) == 0)
    def _():
        carry_ref[...] = jnp.zeros_like(carry_ref)
        ext_ref[0:HALO, :] = jnp.zeros((HALO, CONV_WIDTH), F32)
        ext_ref[HALO + tm:, :] = jnp.zeros((EXT_TAIL, CONV_WIDTH), F32)

    x = x_ref[0]
    shift = mod_ref[0, 0:1, :]
    scale = mod_ref[0, 1:2, :]
    u = (x * (1.0 + scale) + shift).astype(BF16)

    ext_ref[HALO:HALO + tm, :] = _dot(u, wa_ref[...]) * jax.nn.sigmoid(_dot(u, wg_ref[...]))

    def qv_rows(lo, hi):
        return lax.dot_general(wqv_t_ref[lo:hi, :], u, NT_DIMS, preferred_element_type=F32)

    def emit_q(lo, hi):
        qt_ref[0, lo:hi, :] = (qv_rows(lo, hi) * (LOG2E * HEAD_DIM ** -0.5)).astype(BF16)

    def emit_v(lo, hi):
        v_t = qv_rows(ATTN_WIDTH + lo, ATTN_WIDTH + hi).astype(BF16)
        for t in range(tm // TK):
            vt_ref[0, t, lo:hi, :] = v_t[:, t * TK:(t + 1) * TK]

    def emit_k():
        k_ref[0] = _dot(u, wk_ref[...]).astype(BF16)

    def emit_decay():
        fl = _dot(u, wf_ref[...]) + bf_ref[...]
        log_f = jnp.minimum(fl, 0.0) - jnp.log(1.0 + jnp.exp(-jnp.abs(fl)))
        n = CUMSUM_ROWS
        row = lax.broadcasted_iota(jnp.int32, (n, n), 0)
        col = lax.broadcasted_iota(jnp.int32, (n, n), 1)
        tri = jnp.where(row >= col, 1.0, 0.0).astype(BF16)
        lane = lax.broadcasted_iota(jnp.int32, (n, LANES), 1)
        for r in range(0, tm, n):
            h, m, l = _split3(log_f[r:r + n])
            cum = carry_ref[...] + ((_dot(tri, h) + _dot(tri, m)) + _dot(tri, l))
            carry_ref[...] = cum[n - 1:n, :]
            ch, cm, cl = (t.astype(F32) for t in _split3(cum * LOG2E))
            pieces = jnp.where(lane < PIECE_LANE[1], ch,
                               jnp.where(lane < PIECE_LANE[2], cm,
                                         jnp.where(lane < ONE_LANE, cl,
                                                   jnp.where(lane == ONE_LANE, 1.0, 0.0))))
            pieces = pieces.astype(BF16)
            augk_ref[0, r:r + n, :] = _dot(pieces, pselk_ref[...]).astype(BF16)
            augq_t_ref[0, :, r:r + n] = lax.dot_general(
                pselq_t_ref[...], pieces, NT_DIMS, preferred_element_type=F32).astype(BF16)

    half = ATTN_WIDTH // 2
    matmul_items = [lambda: emit_q(0, half), lambda: emit_q(half, ATTN_WIDTH),
                    lambda: emit_v(0, half), lambda: emit_v(half, ATTN_WIDTH),
                    emit_k, emit_decay]

    first = HALO - (CONV_KERNEL - 1)
    rows = tm + 2 * SUBLANES
    y = jnp.zeros((tm, CONV_WIDTH), F32) + bdw_ref[...]
    for b in range(SUBLANES):
        zb = None
        for a in range(pl.cdiv(CONV_KERNEL - b, SUBLANES)):
            k = SUBLANES * a + b
            term = wdw_ref[k:k + 1, :] * ext_ref[SUBLANES * a:SUBLANES * a + rows, :]
            zb = term if zb is None else zb + term
        y = y + zb[first + b:first + b + tm, :]
        if b < len(matmul_items):
            matmul_items[b]()
    ext_ref[0:HALO, :] = ext_ref[tm:tm + HALO, :]
    for r in range(0, tm, ROW_GROUP):
        cn_ref[0, r:r + ROW_GROUP, :] = _norm_conv_branch(y[r:r + ROW_GROUP], gng_ref, gnb_ref,
                                                          gout_ref, gmat_ref)


def _norm_conv_branch(y, gng_ref, gnb_ref, gout_ref, gmat_ref):
    gmat = gmat_ref[...]
    yh, yl = _split2(y)
    mu = _dot(yh, gmat) + _dot(yl, gmat)
    d = y - mu
    dh, dl = _split2(d * d)
    var = _dot(dh, gmat) + _dot(dl, gmat)
    yn = d * lax.rsqrt(var + LN_EPS) * gng_ref[...] + gnb_ref[...]
    sw = yn * jax.nn.sigmoid(yn)
    ms = jnp.mean(sw * sw, axis=-1, keepdims=True)
    return (sw * lax.rsqrt(ms + LN_EPS) * gout_ref[...]).astype(BF16)


def _decay_routing_matrices():
    selq_t = np.zeros((N_HEADS * AUG_ROWS, LANES), np.float32)
    selk = np.zeros((LANES, N_PAIRS * LANES), np.float32)
    for head in range(N_HEADS):
        p, j = divmod(head, 2)
        qbase = head * AUG_ROWS + AUG_PER_HEAD * j
        kbase = p * LANES + AUG_PER_HEAD * j
        for i in range(3):
            selq_t[qbase + i, PIECE_LANE[i] + head] = 1.0
            selq_t[qbase + 3 + i, ONE_LANE] = 1.0
            selk[ONE_LANE, kbase + i] = 1.0
            selk[PIECE_LANE[i] + head, kbase + 3 + i] = -1.0
    return jnp.asarray(selq_t, BF16), jnp.asarray(selk, BF16)


def _inproj(x, mod1, wqv_t, wk, wf, bf, wa, wg, pselq_t, pselk, conv_params, w1, w2):
    b, s, d = x.shape
    tm = TM_PROJ
    steps = s // tm
    const = lambda shape: pl.BlockSpec(shape, lambda bi, i: (0,) * len(shape))
    conv_specs = [const(p.shape) for p in conv_params]
    assert b == 2 and w1.shape[0] % steps == 0 and w2.shape[0] % steps == 0
    w1_spec = pl.BlockSpec((w1.shape[0] // steps, w1.shape[1]),
                           lambda bi, i: (jnp.where(bi == 0, i, steps - 1), 0))
    w2_spec = pl.BlockSpec((w2.shape[0] // steps, w2.shape[1]),
                           lambda bi, i: (jnp.where(bi == 1, i, 0), 0))
    return pl.pallas_call(
        _inproj_kernel,
        out_shape=(jax.ShapeDtypeStruct((b, ATTN_WIDTH, s), BF16),
                   jax.ShapeDtypeStruct((b, s // TK, ATTN_WIDTH, TK), BF16),
                   jax.ShapeDtypeStruct((b, s, ATTN_WIDTH), BF16),
                   jax.ShapeDtypeStruct((b, N_HEADS * AUG_ROWS, s), BF16),
                   jax.ShapeDtypeStruct((b, s, N_PAIRS * LANES), BF16),
                   jax.ShapeDtypeStruct((b, s, CONV_WIDTH), BF16),
                   jax.ShapeDtypeStruct(w1.shape, BF16), jax.ShapeDtypeStruct(w2.shape, BF16)),
        grid=(b, s // tm),
        in_specs=[pl.BlockSpec((1, tm, d), lambda bi, i: (bi, i, 0)),
                  pl.BlockSpec((1, 3, d), lambda bi, i: (bi, 0, 0)),
                  const(wqv_t.shape), const(wk.shape), const(wf.shape), const(bf.shape),
                  const(wa.shape), const(wg.shape), const(pselq_t.shape), const(pselk.shape)]
                 + conv_specs + [w1_spec, w2_spec],
        out_specs=(pl.BlockSpec((1, ATTN_WIDTH, tm), lambda bi, i: (bi, 0, i)),
                   pl.BlockSpec((1, tm // TK, ATTN_WIDTH, TK), lambda bi, i: (bi, i, 0, 0)),
                   pl.BlockSpec((1, tm, ATTN_WIDTH), lambda bi, i: (bi, i, 0)),
                   pl.BlockSpec((1, N_HEADS * AUG_ROWS, tm), lambda bi, i: (bi, 0, i)),
                   pl.BlockSpec((1, tm, N_PAIRS * LANES), lambda bi, i: (bi, i, 0)),
                   pl.BlockSpec((1, tm, CONV_WIDTH), lambda bi, i: (bi, i, 0)),
                   w1_spec, w2_spec),
        scratch_shapes=[pltpu.VMEM((1, LANES), F32),
                        pltpu.VMEM((HALO + tm + EXT_TAIL, CONV_WIDTH), F32)],
        compiler_params=pltpu.CompilerParams(
            dimension_semantics=("arbitrary", "arbitrary"), vmem_limit_bytes=VMEM_LIMIT),
        name="inproj",
    )(x, mod1, wqv_t, wk, wf, bf, wa, wg, pselq_t, pselk, *conv_params, w1, w2)


def _attn_kernel(qt_ref, aqt_ref, k_ref, ak_ref, vt_ref, o_ref, s_scr, acc_scr, st_scr):
    for sub in range(TILES_PER_STEP):
        cols = slice(sub * TQ, (sub + 1) * TQ)
        _attn_tile(pl.program_id(2) * TILES_PER_STEP + sub,
                   qt_ref.at[0, :, cols], aqt_ref.at[0, :, cols], k_ref, ak_ref, vt_ref,
                   o_ref.at[0, cols, :], s_scr.at[sub], acc_scr.at[sub], st_scr.at[sub])


def _attn_tile(qi, qt_ref, aqt_ref, k_ref, ak_ref, vt_ref, o_ref, s_scr, acc_scr, st_scr):
    tq = qt_ref.shape[1]
    assert tq == TK, "the drain handles exactly one diagonal key block"
    heads = range(HEADS_PER_STEP)

    zeros_half = jnp.zeros((HEAD_DIM, tq), BF16)
    zeros_tail = jnp.zeros((LANES - AUG_ROWS, tq), BF16)
    qcat_t = []
    for h in heads:
        qh = qt_ref[HEAD_DIM * h:HEAD_DIM * (h + 1), :]
        halves = [qh, zeros_half] if h % 2 == 0 else [zeros_half, qh]
        qcat_t.append(jnp.concatenate(
            halves + [aqt_ref[AUG_ROWS * h:AUG_ROWS * (h + 1), :], zeros_tail], axis=0))

    def logits(kb, h):
        start = pl.multiple_of(kb * TK, TK)
        pair = pl.ds((h // 2) * LANES, LANES)
        kcat = jnp.concatenate([k_ref[0, pl.ds(start, TK), pair], ak_ref[0, pl.ds(start, TK), pair]],
                               axis=-1)
        return _dot(kcat, qcat_t[h])

    def block_max(s):
        return jnp.max(s, axis=0, keepdims=True)

    def softmax_update(s, m_blk, m):
        m_new = jnp.maximum(m, m_blk)
        alpha = jnp.exp2(m - m_new)
        p = jnp.exp2((s - m_new).astype(BF16))
        return p, alpha, m_new

    ones_rows = jnp.ones((AUG_ROWS, TK), BF16)

    def accumulate(kb, h, p, alpha, acc):
        vt = jnp.concatenate([vt_ref[0, kb, HEAD_DIM * h:HEAD_DIM * (h + 1), :], ones_rows], axis=0)
        return alpha * acc + _dot(vt, p)

    def stat(h, j):
        return st_scr.at[pl.ds(2 * h + j, 1), :]

    for h in heads:
        s0 = logits(0, h)
        s_scr[h] = s0
        acc_scr[h] = jnp.zeros((HEAD_DIM + AUG_ROWS, tq), F32)
        stat(h, 0)[...] = jnp.full((1, tq), NEG_BIG, F32)
        stat(h, 1)[...] = block_max(s0)

    def step(kb):
        for h in heads:
            s_next = logits(kb + 1, h)
            p, alpha, m = softmax_update(s_scr[h], stat(h, 1)[...], stat(h, 0)[...])
            acc_scr[h] = accumulate(kb, h, p, alpha, acc_scr[h])
            s_scr[h] = s_next
            stat(h, 0)[...] = m
            stat(h, 1)[...] = block_max(s_next)

    done = 0
    for unroll in KV_UNROLLS:
        trips = (qi - done) // unroll

        def body(i, carry, unroll=unroll, base=done):
            for u in range(unroll):
                step(base + unroll * i + u)
            return carry

        lax.fori_loop(0, trips, body, 0)
        done = done + trips * unroll

    key = lax.broadcasted_iota(jnp.int32, (TK, tq), 0)
    qry = lax.broadcasted_iota(jnp.int32, (TK, tq), 1)
    outs = []
    for h in heads:
        s = jnp.where(key <= qry, s_scr[h], NEG_BIG)
        p, alpha, _ = softmax_update(s, block_max(s), stat(h, 0)[...])
        acc = accumulate(qi, h, p, alpha, acc_scr[h])
        outs.append(acc[:HEAD_DIM] / acc[HEAD_DIM:HEAD_DIM + 1])
    for pr in range(HEADS_PER_STEP // 2):
        o_ref[:, pr * LANES:(pr + 1) * LANES] = jnp.concatenate(outs[2 * pr:2 * pr + 2], axis=0).T


def _attn(qt, augq_t, k, augk, vt):
    b, s, _ = k.shape
    g, n = HEADS_PER_STEP, TILES_PER_STEP
    tq_step = n * TQ
    return pl.pallas_call(
        _attn_kernel,
        out_shape=jax.ShapeDtypeStruct((b, s, ATTN_WIDTH), F32),
        grid=(b, N_HEADS // g, s // tq_step),
        in_specs=[pl.BlockSpec((1, g * HEAD_DIM, tq_step), lambda bi, p, i: (bi, p, i)),
                  pl.BlockSpec((1, g * AUG_ROWS, tq_step), lambda bi, p, i: (bi, p, i)),
                  pl.BlockSpec((1, s, g * HEAD_DIM), lambda bi, p, i: (bi, 0, p),
                               pipeline_mode=pl.Buffered(1)),
                  pl.BlockSpec((1, s, g * HEAD_DIM), lambda bi, p, i: (bi, 0, p),
                               pipeline_mode=pl.Buffered(1)),
                  pl.BlockSpec((1, s // TK, g * HEAD_DIM, TK), lambda bi, p, i: (bi, 0, p, 0),
                               pipeline_mode=pl.Buffered(1))],
        out_specs=pl.BlockSpec((1, tq_step, g * HEAD_DIM), lambda bi, p, i: (bi, i, p)),
        scratch_shapes=[pltpu.VMEM((n, g, TK, TQ), F32),
                        pltpu.VMEM((n, g, HEAD_DIM + AUG_ROWS, TQ), F32),
                        pltpu.VMEM((n, 2 * g, TQ), F32)],
        compiler_params=pltpu.CompilerParams(
            dimension_semantics=("arbitrary", "arbitrary", "arbitrary"),
            vmem_limit_bytes=VMEM_LIMIT),
        name="attn",
    )(qt, augq_t, k, augk, vt)


def _layernorm(y, g, b):
    mu = jnp.mean(y, axis=-1, keepdims=True)
    d = y - mu
    var = jnp.mean(d * d, axis=-1, keepdims=True)
    return d * lax.rsqrt(var + LN_EPS) * g + b


def _mix_mlp_kernel(attn_ref, cn_ref, x_ref, mod_ref, gattn_ref, wo_a_ref, wo_c_ref,
                    ln1g_ref, ln1b_ref, w1_ref, w2_ref, ln2g_ref, ln2b_ref, o_ref):
    tm = x_ref.shape[1]
    groups = [slice(r, r + ROW_GROUP) for r in range(0, tm, ROW_GROUP)]
    gate1, shift2, scale2, gate2 = (mod_ref[0, j:j + 1, :] for j in range(2, 6))

    def token_mix(rows):
        a = attn_ref[0, rows, :]
        ms = jnp.mean(a * a, axis=-1, keepdims=True)
        an = (a * lax.rsqrt(ms + LN_EPS) * gattn_ref[...]).astype(BF16)
        mixed = _dot(an, wo_a_ref[...]) + _dot(cn_ref[0, rows, :], wo_c_ref[...])
        return DEEPNORM_ALPHA * x_ref[0, rows, :] + (1.0 + gate1) * mixed

    def mlp(x1):
        u = (x1 * (1.0 + scale2) + shift2).astype(BF16)
        ff = jnp.zeros(x1.shape, F32)
        for c in range(0, D_FF, FF_CHUNK):
            hid = jnp.maximum(_dot(u, w1_ref[:, c:c + FF_CHUNK]), 0.0)
            ff = ff + _dot((hid * hid).astype(BF16), w2_ref[c:c + FF_CHUNK, :])
        return DEEPNORM_ALPHA * x1 + (1.0 + gate2) * ff

    x1 = [_layernorm(token_mix(rows), ln1g_ref[...], ln1b_ref[...]) for rows in groups]
    for rows, x1_rows in zip(groups, x1):
        o_ref[0, rows, :] = _layernorm(mlp(x1_rows), ln2g_ref[...], ln2b_ref[...])


def _mix_mlp(attn, cn, x, mods, gattn, wo_a, wo_c, ln1g, ln1b, w1, w2, ln2g, ln2b):
    b, s, d = x.shape
    tm = TM_MLP
    const = lambda a: pl.BlockSpec(a.shape, lambda bi, i: (0,) * a.ndim, pipeline_mode=pl.Buffered(1))
    return pl.pallas_call(
        _mix_mlp_kernel,
        out_shape=jax.ShapeDtypeStruct((b, s, d), F32),
        grid=(b, s // tm),
        in_specs=[pl.BlockSpec((1, tm, ATTN_WIDTH), lambda bi, i: (bi, i, 0)),
                  pl.BlockSpec((1, tm, CONV_WIDTH), lambda bi, i: (bi, i, 0)),
                  pl.BlockSpec((1, tm, d), lambda bi, i: (bi, i, 0)),
                  pl.BlockSpec((1, 6, d), lambda bi, i: (bi, 0, 0)),
                  const(gattn), const(wo_a), const(wo_c), const(ln1g), const(ln1b),
                  const(w1), const(w2), const(ln2g), const(ln2b)],
        out_specs=pl.BlockSpec((1, tm, d), lambda bi, i: (bi, i, 0)),
        compiler_params=pltpu.CompilerParams(
            dimension_semantics=("arbitrary", "arbitrary"), vmem_limit_bytes=VMEM_LIMIT),
        name="mix_mlp",
    )(attn, cn, x, mods, gattn, wo_a, wo_c, ln1g, ln1b, w1, w2, ln2g, ln2b)


def kernel(x, c, w_ada, b_ada, w_in, b_forget, w_dw, b_dw, gn_g, gn_b, g_attn_out, g_conv_out,
           w_out, ln1_g, ln1_b, w_ff1, w_ff2, ln2_g, ln2_b):
    bsz = x.shape[0]
    layer = 0
    row = lambda v: v.reshape(1, -1)

    c_pad = jnp.pad(c, ((0, 2 * SUBLANES - bsz), (0, 0)))
    ada = _ada(c_pad, w_ada[layer], row(b_ada[layer]))[:bsz]
    mods = ada.reshape(bsz, 6, D_MODEL)
    mod1 = mods[:, 0:3]

    w = w_in[layer]
    a0 = 3 * ATTN_WIDTH + N_HEADS
    wq, wk, wv = (w[:, i * ATTN_WIDTH:(i + 1) * ATTN_WIDTH] for i in range(3))
    wqv_t = jnp.concatenate([wq, wv], axis=1).T.astype(BF16)
    wk = wk.astype(BF16)
    reps = LANES // N_HEADS
    wf = jnp.tile(w[:, 3 * ATTN_WIDTH:a0], (1, reps)).astype(BF16)
    bf = jnp.tile(b_forget[layer], reps).reshape(1, LANES)
    wa = w[:, a0:a0 + CONV_WIDTH].astype(BF16)
    wg = w[:, a0 + CONV_WIDTH:].astype(BF16)

    pselq_t, pselk = _decay_routing_matrices()
    grp = np.arange(CONV_WIDTH) // CONV_GROUP
    gmat = jnp.asarray((grp[:, None] == grp[None, :]).astype(np.float32) / CONV_GROUP, BF16)
    conv_params = (w_dw[layer].reshape(CONV_KERNEL, CONV_WIDTH), row(b_dw[layer]),
                   row(gn_g[layer]), row(gn_b[layer]), row(g_conv_out[layer]), gmat)
    qt, vt, k, augq_t, augk, cn, w1_bf, w2_bf = _inproj(
        x, mod1, wqv_t, wk, wf, bf, wa, wg, pselq_t, pselk, conv_params, w_ff1[layer], w_ff2[layer])

    attn = _attn(qt, augq_t, k, augk, vt)

    wo = w_out[layer].astype(BF16)
    return _mix_mlp(attn, cn, x, mods, row(g_attn_out[layer]), wo[:ATTN_WIDTH], wo[ATTN_WIDTH:],
                    row(ln1_g[layer]), row(ln1_b[layer]),
                    w1_bf, w2_bf,
                    row(ln2_g[layer]), row(ln2_b[layer]))
```
